```python
import math
import jax
import jax.numpy as jnp
from jax import lax
import numpy as np

D_MODEL = 4096
BATCH = 32
SEQ = 256
DEPTH = 2
DEC_BATCH = 8
DEC_SEQ = 1024
PAST_LEN = 512

GRID_W = 64
EPS = 1e-6
N_SSM_LAYERS = (DEPTH + 1) // 2
N_ATTN_LAYERS = DEPTH // 2
SSM_EXPAND = 2
D_INNER = SSM_EXPAND * D_MODEL
SSM_HEAD_DIM = 64
SSM_HEADS = D_INNER // SSM_HEAD_DIM
SSM_GROUPS = 8
SSM_STATE = 128
CONV_WIDTH = 3
CONV_DIM = D_INNER + 2 * SSM_GROUPS * SSM_STATE
IN_PROJ_DIM = D_INNER + CONV_DIM + 2 * SSM_HEADS
SSD_CHUNK = 128
HEAD_DIM = 128
N_HEADS = D_MODEL // HEAD_DIM
N_KV_HEADS = N_HEADS // 4
Q_BLOCK = 128
ROPE_THETA = 10000.0
D_FF = 256 * ((8 * D_MODEL + 767) // 768)
N_EXPERTS = 8
TOP_K = 2
D_FF_EXPERT = 7 * D_MODEL // 2
MOE_BLOCK = 128

kernel_name = 'hybrid_diffusion_ssd_gqa_moe_step'


def rmsnorm(x, g):
    xf = x.astype(jnp.float32)
    y = xf * lax.rsqrt(jnp.mean(xf * xf, axis=-1, keepdims=True) + EPS)
    return (y * g.astype(jnp.float32)).astype(x.dtype)


def modulation(cond, mod_w, mod_b):
    m = jax.nn.silu(cond) @ mod_w + mod_b
    return jnp.split(m[:, None, :], 6, axis=-1)


def modulate(h, shift, scale):
    return h * (1 + scale) + shift


def centred_dwconv(x, w, b):
    out = lax.conv_general_dilated(x, w[:, None, :].astype(x.dtype), window_strides=(1,), padding='SAME',
                                   dimension_numbers=('NWC', 'WIO', 'NWC'), feature_group_count=x.shape[-1])
    return out + b


def ssd_scan(x, dt, a, b_in, c_in, h0):
    bsz, seqlen, nh, hp = x.shape
    g, n = b_in.shape[2], b_in.shape[3]
    e = nh // g
    nc = seqlen // SSD_CHUNK
    xs = (x * dt[..., None]).reshape(bsz, nc, SSD_CHUNK, g, e, hp)
    la = jnp.moveaxis((dt * a).reshape(bsz, nc, SSD_CHUNK, g, e), 2, -1)
    cs = jnp.cumsum(la, axis=-1)
    bc = b_in.reshape(bsz, nc, SSD_CHUNK, g, n)
    cc = c_in.reshape(bsz, nc, SSD_CHUNK, g, n)
    lower = jnp.tril(jnp.ones((SSD_CHUNK, SSD_CHUNK), dtype=bool))
    decay = jnp.exp(jnp.where(lower, cs[..., :, None] - cs[..., None, :], -jnp.inf))
    cb = jnp.einsum('bclgn,bcsgn->bcgls', cc, bc)
    scores = cb[:, :, :, None] * decay
    y_diag = jnp.einsum('bcgels,bcsgep->bclgep', scores, xs)
    to_end = jnp.moveaxis(jnp.exp(cs[..., -1:] - cs), -1, 2)[..., None]
    chunk_states = jnp.einsum('bclgn,bclgep->bcgepn', bc, xs * to_end)
    chunk_decay = jnp.exp(cs[..., -1])

    def step(h, inp):
        dec, st = inp
        return h * dec[..., None, None] + st, h

    h_final, h_prev = lax.scan(step, h0.astype(jnp.float32).reshape(bsz, g, e, hp, n),
                               (jnp.moveaxis(chunk_decay, 1, 0), jnp.moveaxis(chunk_states, 1, 0)))
    h_prev = jnp.moveaxis(h_prev, 0, 1)
    y_off = jnp.einsum('bclgn,bcgepn->bclgep', cc, h_prev) * jnp.moveaxis(jnp.exp(cs), -1, 2)[..., None]
    y = (y_diag + y_off).reshape(bsz, seqlen, nh, hp)
    return y, h_final.reshape(bsz, nh, hp, n)


def mamba_mixer(h, h0, in_w, conv_w, conv_b, dt_bias, a_log, d_skip, norm_g, out_w):
    f32 = jnp.float32
    bsz, seqlen, _ = h.shape
    proj = h @ in_w
    z, xbc, dt_raw = jnp.split(proj, [D_INNER, D_INNER + CONV_DIM], axis=-1)
    xbc = jax.nn.silu(centred_dwconv(xbc, conv_w, conv_b))
    xv, bv, cv = jnp.split(xbc, [D_INNER, D_INNER + SSM_GROUPS * SSM_STATE], axis=-1)
    xv = xv.reshape(bsz, seqlen, SSM_HEADS, SSM_HEAD_DIM).astype(f32)
    bv = bv.reshape(bsz, seqlen, SSM_GROUPS, SSM_STATE).astype(f32)
    cv = cv.reshape(bsz, seqlen, SSM_GROUPS, SSM_STATE).astype(f32)
    dt = jax.nn.softplus(dt_raw.astype(f32).reshape(bsz, seqlen, 2, SSM_HEADS) + dt_bias.astype(f32))
    a = -jnp.exp(a_log.astype(f32))
    y_f, hf_f = ssd_scan(xv, dt[:, :, 0], a[0], bv, cv, h0[:, 0])
    y_b, hf_b = ssd_scan(xv[:, ::-1], dt[:, ::-1, 1], a[1], bv[:, ::-1], cv[:, ::-1], h0[:, 1])
    y = y_f + y_b[:, ::-1] + d_skip.astype(f32)[:, None] * xv
    y = y.reshape(bsz, seqlen, D_INNER) * jax.nn.silu(z.astype(f32))
    y = rmsnorm(y.reshape(bsz, seqlen, SSM_GROUPS, D_INNER // SSM_GROUPS),
                norm_g.reshape(SSM_GROUPS, D_INNER // SSM_GROUPS)).reshape(bsz, seqlen, D_INNER)
    return y.astype(h.dtype) @ out_w, jnp.stack([hf_f, hf_b], axis=1)


def rope_angles(n_tokens):
    rows = n_tokens // GRID_W
    t = jnp.arange(rows * GRID_W)
    row = (t // GRID_W).astype(jnp.float32)
    col = (t % GRID_W).astype(jnp.float32)
    axis_dim = HEAD_DIM // 2
    inv_freq = ROPE_THETA ** (-jnp.arange(0, axis_dim, 2, dtype=jnp.float32) / axis_dim)
    return row[:, None] * inv_freq, col[:, None] * inv_freq


def rotate(x, ang):
    x1, x2 = jnp.split(x, 2, axis=-1)
    cos, sin = jnp.cos(ang)[:, None], jnp.sin(ang)[:, None]
    return jnp.concatenate([x1 * cos - x2 * sin, x2 * cos + x1 * sin], axis=-1)


def apply_rope_2d(x, ang_row, ang_col):
    xr, xc = jnp.split(x.astype(jnp.float32), 2, axis=-1)
    return jnp.concatenate([rotate(xr, ang_row), rotate(xc, ang_col)], axis=-1).astype(x.dtype)


def qkv_proj(h, wqkv, q_g, k_g):
    bsz, seqlen, _ = h.shape
    q, k, v = jnp.split(h @ wqkv, [N_HEADS * HEAD_DIM, (N_HEADS + N_KV_HEADS) * HEAD_DIM], axis=-1)
    q = rmsnorm(q.reshape(bsz, seqlen, N_HEADS, HEAD_DIM), q_g)
    k = rmsnorm(k.reshape(bsz, seqlen, N_KV_HEADS, HEAD_DIM), k_g)
    return q, k, v.reshape(bsz, seqlen, N_KV_HEADS, HEAD_DIM)


def blocked_attention(q, k, v):
    bsz, lq = q.shape[0], q.shape[1]
    nb = lq // Q_BLOCK
    grp = N_HEADS // N_KV_HEADS
    qb = q.reshape(bsz, nb, Q_BLOCK, N_KV_HEADS, grp, HEAD_DIM).swapaxes(0, 1)
    scale = HEAD_DIM ** -0.5

    def one_block(qblk):
        s = jnp.einsum('bqkgd,bskd->bkgqs', qblk, k, preferred_element_type=jnp.float32) * scale
        p = jax.nn.softmax(s, axis=-1)
        return jnp.einsum('bkgqs,bskd->bqkgd', p.astype(v.dtype), v)

    out = lax.map(one_block, qb)
    return out.swapaxes(0, 1).reshape(bsz, lq, N_HEADS * HEAD_DIM)


def swiglu(h, w1, w3, w2):
    return (jax.nn.silu(h @ w1) * (h @ w3)) @ w2


def moe_ffn(h, router_w, w1, w3, w2):
    bsz, seqlen, d = h.shape
    x = h.reshape(-1, d)
    n_tok = x.shape[0]
    logits = (x @ router_w).astype(jnp.float32)
    top_logit, top_idx = lax.top_k(logits, TOP_K)
    gates = jax.nn.softmax(top_logit, axis=-1)
    n_slots = n_tok * TOP_K
    flat_e = top_idx.reshape(-1)
    order = jnp.argsort(flat_e)
    e_sorted = flat_e[order]
    counts = jnp.zeros((N_EXPERTS,), jnp.int32).at[flat_e].add(1)
    padded = (counts + MOE_BLOCK - 1) // MOE_BLOCK * MOE_BLOCK
    start = jnp.cumsum(counts) - counts
    pend = jnp.cumsum(padded)
    pstart = pend - padded
    dest = pstart[e_sorted] + (jnp.arange(n_slots, dtype=jnp.int32) - start[e_sorted])
    n_blocks = (n_slots + MOE_BLOCK - 1) // MOE_BLOCK + N_EXPERTS
    cap = n_blocks * MOE_BLOCK
    slot_tok = jnp.full((cap,), n_tok, jnp.int32).at[dest].set((order // TOP_K).astype(jnp.int32))
    slot_gate = jnp.zeros((cap,), jnp.float32).at[dest].set(gates.reshape(-1)[order])
    block_start = jnp.arange(n_blocks, dtype=jnp.int32) * MOE_BLOCK
    block_expert = jnp.minimum(jnp.sum(block_start[:, None] >= pend[None, :], axis=1), N_EXPERTS - 1)
    x_pad = jnp.concatenate([x, jnp.zeros((1, d), x.dtype)], axis=0)

    def run_block(args):
        tok, e = args
        xb = x_pad[tok]
        return swiglu(xb, w1[e], w3[e], w2[e])

    out = lax.map(run_block, (slot_tok.reshape(n_blocks, MOE_BLOCK), block_expert))
    out = out.reshape(cap, d) * slot_gate[:, None].astype(out.dtype)
    y = jnp.zeros((n_tok + 1, d), out.dtype).at[slot_tok].add(out)[:n_tok]
    return y.reshape(bsz, seqlen, d)


def setup_inputs(seed: int = 0) -> dict:
    key = jax.random.key(seed)
    ks = jax.random.split(key, 30)
    f32 = jnp.float32

    def nrm(k, shape, scale):
        return jax.random.normal(k, shape, f32) * scale

    def gain(k, shape):
        return 1.0 + 0.05 * jax.random.normal(k, shape, f32)

    dt0 = jnp.exp(jax.random.uniform(ks[10], (N_SSM_LAYERS, 2, SSM_HEADS), f32, math.log(1e-3), math.log(1e-1)))
    qkv_dim = (N_HEADS + 2 * N_KV_HEADS) * HEAD_DIM
    return {
        'x_prompt': nrm(ks[0], (BATCH, SEQ, D_MODEL), 1.0),
        'x_sample': nrm(ks[1], (DEC_BATCH, DEC_SEQ, D_MODEL), 1.0),
        'state_ssm': nrm(ks[2], (DEC_BATCH, N_SSM_LAYERS, 2, SSM_HEADS, SSM_HEAD_DIM, SSM_STATE), 0.1),
        'cache_k': nrm(ks[3], (DEC_BATCH, N_ATTN_LAYERS, PAST_LEN, N_KV_HEADS, HEAD_DIM), 1.0),
        'cache_v': nrm(ks[4], (DEC_BATCH, N_ATTN_LAYERS, PAST_LEN, N_KV_HEADS, HEAD_DIM), 1.0),
        'c': nrm(ks[5], (DEC_BATCH, D_MODEL), 1.0),
        'c_ctx': nrm(ks[6], (D_MODEL,), 1.0),
        'mod_w': nrm(ks[7], (DEPTH, D_MODEL, 6 * D_MODEL), 0.5 * D_MODEL ** -0.5),
        'mod_b': nrm(ks[8], (DEPTH, 6 * D_MODEL), 0.02),
        'norm_w': gain(ks[9], (DEPTH, 2, D_MODEL)),
        'ssm_in_w': nrm(ks[11], (N_SSM_LAYERS, D_MODEL, IN_PROJ_DIM), D_MODEL ** -0.5),
        'ssm_conv_w': nrm(ks[12], (N_SSM_LAYERS, CONV_WIDTH, CONV_DIM), CONV_WIDTH ** -0.5),
        'ssm_conv_b': nrm(ks[13], (N_SSM_LAYERS, CONV_DIM), 0.02),
        'ssm_dt_bias': dt0 + jnp.log(-jnp.expm1(-dt0)),
        'ssm_a_log': jnp.log(jax.random.uniform(ks[14], (N_SSM_LAYERS, 2, SSM_HEADS), f32, 1.0, 16.0)),
        'ssm_d': gain(ks[15], (N_SSM_LAYERS, SSM_HEADS)),
        'ssm_norm': gain(ks[16], (N_SSM_LAYERS, D_INNER)),
        'ssm_out_w': nrm(ks[17], (N_SSM_LAYERS, D_INNER, D_MODEL), D_INNER ** -0.5),
        'attn_wqkv': nrm(ks[18], (N_ATTN_LAYERS, D_MODEL, qkv_dim), D_MODEL ** -0.5),
        'attn_q_norm': gain(ks[19], (N_ATTN_LAYERS, HEAD_DIM)),
        'attn_k_norm': gain(ks[20], (N_ATTN_LAYERS, HEAD_DIM)),
        'attn_wo': nrm(ks[21], (N_ATTN_LAYERS, N_HEADS * HEAD_DIM, D_MODEL), (N_HEADS * HEAD_DIM) ** -0.5),
        'ffn_w1': nrm(ks[22], (N_SSM_LAYERS, D_MODEL, D_FF), D_MODEL ** -0.5),
        'ffn_w3': nrm(ks[23], (N_SSM_LAYERS, D_MODEL, D_FF), D_MODEL ** -0.5),
        'ffn_w2': nrm(ks[24], (N_SSM_LAYERS, D_FF, D_MODEL), D_FF ** -0.5),
        'moe_router': nrm(ks[25], (N_ATTN_LAYERS, D_MODEL, N_EXPERTS), D_MODEL ** -0.5),
        'moe_w1': nrm(ks[26], (N_ATTN_LAYERS, N_EXPERTS, D_MODEL, D_FF_EXPERT), D_MODEL ** -0.5),
        'moe_w3': nrm(ks[27], (N_ATTN_LAYERS, N_EXPERTS, D_MODEL, D_FF_EXPERT), D_MODEL ** -0.5),
        'moe_w2': nrm(ks[28], (N_ATTN_LAYERS, N_EXPERTS, D_FF_EXPERT, D_MODEL), D_FF_EXPERT ** -0.5),
        'final_norm': gain(ks[29], (D_MODEL,)),
    }


def reference(x_prompt, x_sample, state_ssm, cache_k, cache_v, c, c_ctx, mod_w, mod_b, norm_w,
              ssm_in_w, ssm_conv_w, ssm_conv_b, ssm_dt_bias, ssm_a_log, ssm_d, ssm_norm, ssm_out_w,
              attn_wqkv, attn_q_norm, attn_k_norm, attn_wo, ffn_w1, ffn_w3, ffn_w2,
              moe_router, moe_w1, moe_w3, moe_w2, final_norm):
    xp, xs = x_prompt, x_sample
    ang_row, ang_col = rope_angles(xs.shape[1])
    zero_state = jnp.zeros((xp.shape[0], 2, SSM_HEADS, SSM_HEAD_DIM, SSM_STATE), jnp.float32)
    new_ssm, new_k, new_v = [], [], []
    for i in range(DEPTH):
        j = i // 2
        mp = modulation(c_ctx[None, :], mod_w[i], mod_b[i])
        ms = modulation(c, mod_w[i], mod_b[i])
        hp = modulate(rmsnorm(xp, norm_w[i, 0]), mp[0], mp[1])
        hs = modulate(rmsnorm(xs, norm_w[i, 0]), ms[0], ms[1])
        if i % 2 == 0:
            ssm_args = (ssm_in_w[j], ssm_conv_w[j], ssm_conv_b[j], ssm_dt_bias[j], ssm_a_log[j],
                        ssm_d[j], ssm_norm[j], ssm_out_w[j])
            op, st = mamba_mixer(hp, zero_state, *ssm_args)
            new_ssm.append(st)
            os_, _ = mamba_mixer(hs, state_ssm[:, j], *ssm_args)
        else:
            qp, kp, vp = qkv_proj(hp, attn_wqkv[j], attn_q_norm[j], attn_k_norm[j])
            op = blocked_attention(qp, kp, vp) @ attn_wo[j]
            new_k.append(kp)
            new_v.append(vp)
            qs, ks_, vs = qkv_proj(hs, attn_wqkv[j], attn_q_norm[j], attn_k_norm[j])
            qs = apply_rope_2d(qs, ang_row, ang_col)
            ks_ = apply_rope_2d(ks_, ang_row, ang_col)
            k_all = jnp.concatenate([ks_, cache_k[:, j].astype(ks_.dtype)], axis=1)
            v_all = jnp.concatenate([vs, cache_v[:, j].astype(vs.dtype)], axis=1)
            os_ = blocked_attention(qs, k_all, v_all) @ attn_wo[j]
        xp = xp + mp[2] * op
        xs = xs + ms[2] * os_
        hp = modulate(rmsnorm(xp, norm_w[i, 1]), mp[3], mp[4])
        hs = modulate(rmsnorm(xs, norm_w[i, 1]), ms[3], ms[4])
        if i % 2 == 0:
            fp = swiglu(hp, ffn_w1[j], ffn_w3[j], ffn_w2[j])
            fs = swiglu(hs, ffn_w1[j], ffn_w3[j], ffn_w2[j])
        else:
            fp = moe_ffn(hp, moe_router[j], moe_w1[j], moe_w3[j], moe_w2[j])
            fs = moe_ffn(hs, moe_router[j], moe_w1[j], moe_w3[j], moe_w2[j])
        xp = xp + mp[5] * fp
        xs = xs + ms[5] * fs
    y_prompt = rmsnorm(xp, final_norm)
    y_sample = rmsnorm(xs, final_norm)
    new_state_ssm = jnp.stack(new_ssm, axis=1)
    new_cache_k = jnp.stack(new_k, axis=1)
    new_cache_v = jnp.stack(new_v, axis=1)
    return (y_prompt, y_sample, new_state_ssm, new_cache_k, new_cache_v)
```

```python
import functools

import numpy as np
import jax
import jax.numpy as jnp
from jax import lax
from jax.experimental import pallas as pl
from jax.experimental.pallas import tpu as pltpu

F32 = jnp.float32
BF16 = jnp.bfloat16

EPS = 1e-6
GRID_W = 64
ROPE_THETA = 10000.0
HEAD_DIM = 128
KV_GROUP = 4
SSM_HEAD_DIM = 64
SSM_GROUPS = 8
SSM_STATE = 128
SSD_CHUNK = 128
N_EXPERTS = 8
TOP_K = 2
LANES = 128
COND_ROWS = 16
VMEM_LIMIT_BYTES = 56 * 1024 * 1024


def _params(*sem):
    return pltpu.CompilerParams(dimension_semantics=sem, vmem_limit_bytes=VMEM_LIMIT_BYTES)


def _silu(x):
    return x * jax.nn.sigmoid(x)


def _bdot(a, b):
    return jnp.dot(a.astype(BF16), b.astype(BF16), preferred_element_type=F32)


def _pick(n, prefs):
    for t in prefs:
        if n % t == 0:
            return t
    return n


class _Tokens:
    def __init__(self, n_ctx_seq, ctx_len, n_lat_seq, lat_len):
        self.n_ctx_seq, self.ctx_len = n_ctx_seq, ctx_len
        self.n_lat_seq, self.lat_len = n_lat_seq, lat_len
        self.n_ctx = n_ctx_seq * ctx_len
        self.n_lat = n_lat_seq * lat_len
        self.total = self.n_ctx + self.n_lat
        assert self.n_ctx % lat_len == 0 and lat_len % ctx_len == 0
        assert ctx_len & (ctx_len - 1) == 0 and lat_len & (lat_len - 1) == 0

    def cond_of_tile(self, i, tm):
        assert self.ctx_len % tm == 0 or tm % self.ctx_len == 0
        assert self.lat_len % tm == 0 and self.n_ctx % tm == 0
        row = i * tm
        return jnp.where(row < self.n_ctx, self.n_lat_seq, (row - self.n_ctx) // self.lat_len)


def _mod_kernel(c_ref, w_ref, b_ref, o_ref):
    o_ref[...] = _bdot(_silu(c_ref[...]), w_ref[...]) + b_ref[...]


def _modulation(cond, mod_w, mod_b):
    depth, d, n = mod_w.shape
    tn = _pick(n, (512, 256, 128))
    out = pl.pallas_call(
        _mod_kernel,
        grid=(depth, n // tn),
        in_specs=[
            pl.BlockSpec((COND_ROWS, d), lambda l, j: (0, 0)),
            pl.BlockSpec((None, d, tn), lambda l, j: (l, 0, j)),
            pl.BlockSpec((None, 1, tn), lambda l, j: (l, 0, j)),
        ],
        out_specs=pl.BlockSpec((None, COND_ROWS, tn), lambda l, j: (l, 0, j)),
        out_shape=jax.ShapeDtypeStruct((depth, COND_ROWS, n), F32),
        compiler_params=_params("arbitrary", "arbitrary"),
        name="modulation",
    )(cond, mod_w, mod_b.reshape(depth, 1, n))
    return out.reshape(depth, COND_ROWS, 1, n)


def _norm_mod_kernel(x_ref, g_ref, sh_ref, sc_ref, o_ref):
    x = x_ref[...]
    y = x * lax.rsqrt(jnp.mean(x * x, axis=-1, keepdims=True) + EPS) * g_ref[...]
    o_ref[...] = (y * (1.0 + sc_ref[...]) + sh_ref[...]).astype(o_ref.dtype)


def _norm_mod(x, norm_w4, mods, tok, layer, which, out_dtype):
    t, d = x.shape
    tm = _pick(tok.ctx_len, (256, 128))
    cond = lambda i: tok.cond_of_tile(i, tm)
    return pl.pallas_call(
        _norm_mod_kernel,
        grid=(t // tm,),
        in_specs=[
            pl.BlockSpec((tm, d), lambda i: (i, 0)),
            pl.BlockSpec((None, 1, d), lambda i: (2 * layer + which, 0, 0)),
            pl.BlockSpec((None, None, 1, d), lambda i: (layer, cond(i), 0, 3 * which)),
            pl.BlockSpec((None, None, 1, d), lambda i: (layer, cond(i), 0, 3 * which + 1)),
        ],
        out_specs=pl.BlockSpec((tm, d), lambda i: (i, 0)),
        out_shape=jax.ShapeDtypeStruct((t, d), out_dtype),
        compiler_params=_params("arbitrary"),
        name="norm_mod",
    )(x, norm_w4, mods, mods)


def _mm_kernel(x_ref, w_ref, o_ref):
    o_ref[...] = _bdot(x_ref[...], w_ref[...]).astype(o_ref.dtype)


def _mm_resid_kernel(x_ref, w_ref, r_ref, g_ref, o_ref):
    o_ref[...] = r_ref[...] + g_ref[...] * _bdot(x_ref[...], w_ref[...])


def _mm_swiglu_kernel(x_ref, w1_ref, w3_ref, o_ref):
    x = x_ref[...]
    o_ref[...] = (_silu(_bdot(x, w1_ref[...])) * _bdot(x, w3_ref[...])).astype(o_ref.dtype)


def _matmul(x, w, n_cols, col0, out_dtype, tm, tn, name):
    t, k = x.shape
    assert t % tm == 0 and n_cols % tn == 0 and col0 % tn == 0
    cb = col0 // tn
    return pl.pallas_call(
        _mm_kernel,
        grid=(t // tm, n_cols // tn),
        in_specs=[
            pl.BlockSpec((tm, k), lambda i, j: (i, 0)),
            pl.BlockSpec((k, tn), lambda i, j: (0, cb + j)),
        ],
        out_specs=pl.BlockSpec((tm, tn), lambda i, j: (i, j)),
        out_shape=jax.ShapeDtypeStruct((t, n_cols), out_dtype),
        compiler_params=_params("arbitrary", "arbitrary"),
        name=name,
    )(x, w)


def _matmul_resid(x, w, res, mods, tok, layer, chunk, tm, tn, name):
    t, k = x.shape
    n = w.shape[1]
    assert t % tm == 0 and n % tn == 0
    nb = n // tn
    cond = lambda i: tok.cond_of_tile(i, tm)
    return pl.pallas_call(
        _mm_resid_kernel,
        grid=(t // tm, nb),
        in_specs=[
            pl.BlockSpec((tm, k), lambda i, j: (i, 0)),
            pl.BlockSpec((k, tn), lambda i, j: (0, j)),
            pl.BlockSpec((tm, tn), lambda i, j: (i, j)),
            pl.BlockSpec((None, None, 1, tn), lambda i, j: (layer, cond(i), 0, chunk * nb + j)),
        ],
        out_specs=pl.BlockSpec((tm, tn), lambda i, j: (i, j)),
        out_shape=jax.ShapeDtypeStruct((t, n), F32),
        compiler_params=_params("arbitrary", "arbitrary"),
        name=name,
    )(x, w, res, mods)


def _matmul_swiglu(x, w1, w3, tm, tn, name):
    t, k = x.shape
    n = w1.shape[1]
    assert t % tm == 0 and n % tn == 0
    return pl.pallas_call(
        _mm_swiglu_kernel,
        grid=(t // tm, n // tn),
        in_specs=[
            pl.BlockSpec((tm, k), lambda i, j: (i, 0)),
            pl.BlockSpec((k, tn), lambda i, j: (0, j)),
            pl.BlockSpec((k, tn), lambda i, j: (0, j)),
        ],
        out_specs=pl.BlockSpec((tm, tn), lambda i, j: (i, j)),
        out_shape=jax.ShapeDtypeStruct((t, n), BF16),
        compiler_params=_params("arbitrary", "arbitrary"),
        name=name,
    )(x, w1, w3)


def _conv_kernel(x_ref, w_ref, b_ref, o_ref, *, tok):
    i = pl.program_id(0)
    x = x_ref[...]
    rows = x.shape[0]
    seglen = jnp.where(i * rows < tok.n_ctx, tok.ctx_len, tok.lat_len)
    t = lax.broadcasted_iota(jnp.int32, x.shape, 0) & (seglen - 1)
    prev = jnp.where(t == 0, 0.0, pltpu.roll(x, 1, 0))
    nxt = jnp.where(t == seglen - 1, 0.0, pltpu.roll(x, rows - 1, 0))
    w = w_ref[...]
    y = prev * w[0:1] + x * w[1:2] + nxt * w[2:3] + b_ref[...]
    o_ref[...] = _silu(y)


def _conv_silu(proj, col0, conv_w, conv_b, tok):
    t = proj.shape[0]
    n = conv_w.shape[1]
    rows = tok.lat_len
    tc = _pick(n, (512, 256, 128))
    assert col0 % tc == 0
    cb = col0 // tc
    return pl.pallas_call(
        functools.partial(_conv_kernel, tok=tok),
        grid=(t // rows, n // tc),
        in_specs=[
            pl.BlockSpec((rows, tc), lambda i, j: (i, cb + j)),
            pl.BlockSpec((3, tc), lambda i, j: (0, j)),
            pl.BlockSpec((1, tc), lambda i, j: (0, j)),
        ],
        out_specs=pl.BlockSpec((rows, tc), lambda i, j: (i, j)),
        out_shape=jax.ShapeDtypeStruct((t, n), F32),
        compiler_params=_params("arbitrary", "arbitrary"),
        name="conv_silu",
    )(proj, conv_w, conv_b)


def _split3_dot(mat, v):
    hi = v.astype(BF16)
    r1 = v - hi.astype(F32)
    mid = r1.astype(BF16)
    lo = (r1 - mid.astype(F32)).astype(BF16)
    dot = lambda b: jnp.dot(mat, b, preferred_element_type=F32)
    return dot(hi) + dot(mid) + dot(lo)


def _dt_kernel(raw_ref, bias_ref, alog_ref, dt_ref, cs_ref, *, heads):
    x = raw_ref[...] + bias_ref[...]
    dt = jnp.maximum(x, 0.0) + jnp.log1p(jnp.exp(-jnp.abs(x)))
    la = dt * (-jnp.exp(alog_ref[...]))
    r = lax.broadcasted_iota(jnp.int32, (SSD_CHUNK, SSD_CHUNK), 0)
    c = lax.broadcasted_iota(jnp.int32, (SSD_CHUNK, SSD_CHUNK), 1)
    lower = jnp.where(r >= c, 1.0, 0.0).astype(BF16)
    upper = jnp.where(r <= c, 1.0, 0.0).astype(BF16)
    dt_ref[...] = dt
    cs_ref[:, :heads] = _split3_dot(lower, la[:, :heads])
    cs_ref[:, heads:] = _split3_dot(upper, la[:, heads:])


def _dt_prep(dt_raw, dt_bias, a_log):
    t, w = dt_raw.shape
    spec = pl.BlockSpec((SSD_CHUNK, w), lambda i: (i, 0))
    par = pl.BlockSpec((1, w), lambda i: (0, 0))
    return pl.pallas_call(
        functools.partial(_dt_kernel, heads=w // 2),
        grid=(t // SSD_CHUNK,),
        in_specs=[spec, par, par],
        out_specs=[spec, spec],
        out_shape=[jax.ShapeDtypeStruct((t, w), F32)] * 2,
        compiler_params=_params("arbitrary"),
        name="dt_prep",
    )(dt_raw, dt_bias, a_log)


def _ssd_direction(d, x_ref, b_ref, c_ref, dt_ref, cs_ref, cst_ref, y_ref, h_scr, s_scr, heads):
    p = SSM_HEAD_DIM
    bmat = b_ref[...]
    cmat = c_ref[...].astype(BF16)
    cb = lax.dot_general(cmat, bmat.astype(BF16), (((1,), (1,)), ((), ())), preferred_element_type=F32)
    r = lax.broadcasted_iota(jnp.int32, (SSD_CHUNK, SSD_CHUNK), 0)
    c = lax.broadcasted_iota(jnp.int32, (SSD_CHUNK, SSD_CHUNK), 1)
    mask = (r >= c) if d == 0 else (r <= c)
    end = SSD_CHUNK - 1 if d == 0 else 0
    cs = cs_ref[...]
    cst = cst_ref[...]
    dt = dt_ref[...]
    y_off = jnp.dot(cmat, h_scr[d].astype(BF16), preferred_element_type=F32)
    for e in range(heads):
        lanes = slice(e * p, (e + 1) * p)
        col = cs[:, e:e + 1]
        decay = jnp.where(mask, jnp.exp(jnp.minimum(col - cst[e:e + 1, :], 0.0)), 0.0)
        xs = x_ref[:, lanes] * dt[:, e:e + 1]
        y_diag = jnp.dot((cb * decay).astype(BF16), xs.astype(BF16), preferred_element_type=F32)
        y_ref[:, lanes] = y_diag + y_off[:, lanes] * jnp.exp(col)
        s_scr[:, lanes] = xs * jnp.exp(cs[end:end + 1, e:e + 1] - col)
    new = jnp.dot(bmat.T.astype(BF16), s_scr[...].astype(BF16), preferred_element_type=F32)
    for e in range(heads):
        lanes = slice(e * p, (e + 1) * p)
        h_scr[d, :, lanes] = h_scr[d, :, lanes] * jnp.exp(cs[end:end + 1, e:e + 1]) + new[:, lanes]


def _ssd_kernel(fr_ref, br_ref, first_ref, last_ref, seq_ref, lat_ref,
                xf_ref, bf_ref, cf_ref, dtf_ref, csf_ref, cstf_ref,
                xb_ref, bb_ref, cb_ref, dtb_ref, csb_ref, cstb_ref,
                h0_ref, yf_ref, yb_ref, hfin_ref, h_scr, s_scr, *, heads):
    s = pl.program_id(1)

    @pl.when(first_ref[s] == 1)
    def _():
        h_scr[...] = jnp.where(lat_ref[s] == 1, h0_ref[...], 0.0)

    _ssd_direction(0, xf_ref, bf_ref, cf_ref, dtf_ref, csf_ref, cstf_ref, yf_ref, h_scr, s_scr, heads)
    _ssd_direction(1, xb_ref, bb_ref, cb_ref, dtb_ref, csb_ref, cstb_ref, yb_ref, h_scr, s_scr, heads)

    @pl.when(last_ref[s] == 1)
    def _():
        hfin_ref[...] = h_scr[...]


def _ssd_steps(tok):
    rows = []
    chunk = 0
    for n_seq, length, lat in ((tok.n_ctx_seq, tok.ctx_len, 0), (tok.n_lat_seq, tok.lat_len, 1)):
        nc = length // SSD_CHUNK
        for b in range(n_seq):
            for c in range(nc):
                slot = b if lat else b
                rows.append((chunk + c, chunk + nc - 1 - c, int(c == 0), int(c == nc - 1), slot, lat))
            chunk += nc
    return [jnp.asarray(np.array(col, np.int32)) for col in zip(*rows)]


def _ssd(xbc, dtg, csg, cstg, h0, tok, d_inner):
    t = xbc.shape[0]
    g, n = SSM_GROUPS, SSM_STATE
    gw = d_inner // g
    heads = gw // SSM_HEAD_DIM
    assert n == SSD_CHUNK and gw % n == 0
    b0 = d_inner // n
    c0 = b0 + g
    steps = _ssd_steps(tok)
    n_steps = steps[0].shape[0]
    n_fin = tok.n_ctx_seq + 1

    def spec(shape, fn):
        return pl.BlockSpec(shape, fn)

    def dir_specs(d):
        row = (lambda s, fr, br: fr[s]) if d == 0 else (lambda s, fr, br: br[s])
        return [
            spec((SSD_CHUNK, gw), lambda gi, s, fr, br, *_: (row(s, fr, br), gi)),
            spec((SSD_CHUNK, n), lambda gi, s, fr, br, *_: (row(s, fr, br), b0 + gi)),
            spec((SSD_CHUNK, n), lambda gi, s, fr, br, *_: (row(s, fr, br), c0 + gi)),
            spec((None, None, SSD_CHUNK, heads), lambda gi, s, fr, br, *_: (d, gi, row(s, fr, br), 0)),
            spec((None, None, SSD_CHUNK, heads), lambda gi, s, fr, br, *_: (d, gi, row(s, fr, br), 0)),
            spec((None, None, heads, SSD_CHUNK), lambda gi, s, fr, br, *_: (d, gi, 0, row(s, fr, br))),
        ]

    def h0_map(gi, s, fr, br, first, last, seq, lat):
        return (jnp.where(lat[s] == 1, seq[s], 0), 0, gi, 0, 0)

    def hfin_map(gi, s, fr, br, first, last, seq, lat):
        return (jnp.where(lat[s] == 1, n_fin - 1, seq[s]), 0, gi, 0, 0)

    state_block = (None, 2, None, n, gw)
    grid_spec = pltpu.PrefetchScalarGridSpec(
        num_scalar_prefetch=6,
        grid=(g, n_steps),
        in_specs=dir_specs(0) + dir_specs(1) + [spec(state_block, h0_map)],
        out_specs=[
            spec((SSD_CHUNK, gw), lambda gi, s, fr, br, *_: (fr[s], gi)),
            spec((SSD_CHUNK, gw), lambda gi, s, fr, br, *_: (br[s], gi)),
            spec(state_block, hfin_map),
        ],
        scratch_shapes=[pltpu.VMEM((2, n, gw), F32), pltpu.VMEM((SSD_CHUNK, gw), F32)],
    )
    return pl.pallas_call(
        functools.partial(_ssd_kernel, heads=heads),
        grid_spec=grid_spec,
        out_shape=[
            jax.ShapeDtypeStruct((t, d_inner), F32),
            jax.ShapeDtypeStruct((t, d_inner), F32),
            jax.ShapeDtypeStruct((n_fin, 2, g, n, gw), F32),
        ],
        compiler_params=_params("arbitrary", "arbitrary"),
        name="ssd_scan",
    )(*steps, xbc, xbc, xbc, dtg, csg, cstg, xbc, xbc, xbc, dtg, csg, cstg, h0)


def _gate_norm_kernel(yf_ref, yb_ref, xv_ref, z_ref, d_ref, g_ref, o_ref):
    y = yf_ref[...] + yb_ref[...] + d_ref[...] * xv_ref[...]
    y = y * _silu(z_ref[...])
    y = y * lax.rsqrt(jnp.mean(y * y, axis=-1, keepdims=True) + EPS) * g_ref[...]
    o_ref[...] = y.astype(o_ref.dtype)


def _gate_norm(y_f, y_b, xbc, proj, d_row, norm_row):
    t, d_inner = y_f.shape
    gw = d_inner // SSM_GROUPS
    tm = 256
    blk = pl.BlockSpec((tm, gw), lambda i, j: (i, j))
    par = pl.BlockSpec((1, gw), lambda i, j: (0, j))
    return pl.pallas_call(
        _gate_norm_kernel,
        grid=(t // tm, SSM_GROUPS),
        in_specs=[blk, blk, blk, blk, par, par],
        out_specs=blk,
        out_shape=jax.ShapeDtypeStruct((t, d_inner), BF16),
        compiler_params=_params("arbitrary", "arbitrary"),
        name="gate_norm",
    )(y_f, y_b, xbc, proj, d_row, norm_row)


def _qk_norm_rope_kernel(x_ref, g_ref, cos_ref, sin_ref, o_ref, *rest, n_ctx_tiles):
    i = pl.program_id(0)
    x = x_ref[...]
    y = x * lax.rsqrt(jnp.mean(x * x, axis=-1, keepdims=True) + EPS) * g_ref[...]
    if rest:
        rest[0][...] = y
    lane = lax.broadcasted_iota(jnp.int32, y.shape, 1)
    low = (lane & (HEAD_DIM // 2 - 1)) < HEAD_DIM // 4
    partner = jnp.where(low, pltpu.roll(y, HEAD_DIM - HEAD_DIM // 4, 1), pltpu.roll(y, HEAD_DIM // 4, 1))
    rot = y * cos_ref[...] + partner * sin_ref[...]
    o_ref[...] = jnp.where(i >= n_ctx_tiles, rot, y).astype(o_ref.dtype)


def _qk_norm_rope(qkv, col0, n_heads, gain, cos, sin, tok, emit_f32):
    t = qkv.shape[0]
    rows = tok.lat_len
    hb = col0 // HEAD_DIM
    blk_in = pl.BlockSpec((rows, HEAD_DIM), lambda i, j: (i, hb + j))
    blk = pl.BlockSpec((rows, HEAD_DIM), lambda i, j: (i, j))
    par = pl.BlockSpec((1, HEAD_DIM), lambda i, j: (0, 0))
    tab = pl.BlockSpec((rows, HEAD_DIM), lambda i, j: (0, 0))
    shapes = [jax.ShapeDtypeStruct((t, n_heads * HEAD_DIM), BF16)]
    if emit_f32:
        shapes.append(jax.ShapeDtypeStruct((t, n_heads * HEAD_DIM), F32))
    return pl.pallas_call(
        functools.partial(_qk_norm_rope_kernel, n_ctx_tiles=tok.n_ctx // rows),
        grid=(t // rows, n_heads),
        in_specs=[blk_in, par, tab, tab],
        out_specs=[blk] * len(shapes),
        out_shape=shapes,
        compiler_params=_params("arbitrary", "arbitrary"),
        name="qk_norm_rope",
    )(qkv, gain, cos, sin)


def _attn_kernel(q_ref, k_ref, v_ref, *rest, has_cache):
    if has_cache:
        kc_ref, vc_ref, o_ref = rest
    else:
        (o_ref,) = rest
    scale = HEAD_DIM ** -0.5
    nt = (((1,), (1,)), ((), ()))
    k = k_ref[...]
    v = v_ref[...].astype(BF16)
    if has_cache:
        kc = kc_ref[...].astype(BF16)
        vc = vc_ref[...].astype(BF16)
    for h in range(KV_GROUP):
        lanes = slice(h * HEAD_DIM, (h + 1) * HEAD_DIM)
        q = q_ref[:, lanes]
        s = lax.dot_general(q, k, nt, preferred_element_type=F32) * scale
        m = jnp.max(s, axis=-1, keepdims=True)
        if has_cache:
            sc = lax.dot_general(q, kc, nt, preferred_element_type=F32) * scale
            m = jnp.maximum(m, jnp.max(sc, axis=-1, keepdims=True))
            ec = jnp.exp(sc - m)
        e = jnp.exp(s - m)
        den = jnp.sum(e, axis=-1, keepdims=True)
        if has_cache:
            den = den + jnp.sum(ec, axis=-1, keepdims=True)
        inv = 1.0 / den
        acc = jnp.dot((e * inv).astype(BF16), v, preferred_element_type=F32)
        if has_cache:
            acc = acc + jnp.dot((ec * inv).astype(BF16), vc, preferred_element_type=F32)
        o_ref[:, lanes] = acc.astype(o_ref.dtype)


def _attention(q, k, qkv, v_col0, row0, n_seq, seq_len, n_kv, cache_k=None, cache_v=None):
    tq = _pick(seq_len, (256, 128))
    nq = seq_len // tq
    sb = row0 // seq_len
    vb = v_col0 // HEAD_DIM
    qw = KV_GROUP * HEAD_DIM
    in_specs = [
        pl.BlockSpec((tq, qw), lambda b, h, i: ((sb + b) * nq + i, h)),
        pl.BlockSpec((seq_len, HEAD_DIM), lambda b, h, i: (sb + b, h)),
        pl.BlockSpec((seq_len, HEAD_DIM), lambda b, h, i: (sb + b, vb + h)),
    ]
    args = [q, k, qkv]
    if cache_k is not None:
        past = cache_k.shape[1]
        cspec = pl.BlockSpec((None, past, HEAD_DIM), lambda b, h, i: (b, 0, h))
        in_specs += [cspec, cspec]
        args += [cache_k, cache_v]
    return pl.pallas_call(
        functools.partial(_attn_kernel, has_cache=cache_k is not None),
        grid=(n_seq, n_kv, nq),
        in_specs=in_specs,
        out_specs=pl.BlockSpec((tq, qw), lambda b, h, i: (b * nq + i, h)),
        out_shape=jax.ShapeDtypeStruct((n_seq * seq_len, n_kv * qw), BF16),
        compiler_params=_params("arbitrary", "arbitrary", "arbitrary"),
        name="attention",
    )(*args)


def _rope_tables(n_tokens):
    t = jnp.arange(n_tokens)
    row = (t // GRID_W).astype(F32)
    col = (t % GRID_W).astype(F32)
    axis_dim = HEAD_DIM // 2
    inv_freq = ROPE_THETA ** (-jnp.arange(0, axis_dim, 2, dtype=F32) / axis_dim)
    ang = jnp.concatenate([row[:, None] * inv_freq] * 2 + [col[:, None] * inv_freq] * 2, axis=-1)
    sign = jnp.tile(jnp.repeat(jnp.array([-1.0, 1.0], F32), HEAD_DIM // 4), 2)
    return jnp.cos(ang), jnp.sin(ang) * sign


def _router_kernel(x_ref, w_ref, idx_ref, gate_ref):
    logits = _bdot(x_ref[...], w_ref[...])
    lane = lax.broadcasted_iota(jnp.int32, logits.shape, 1)
    neg = -jnp.inf
    l1 = jnp.where(lane < N_EXPERTS, logits, neg)
    m1 = jnp.max(l1, axis=-1, keepdims=True)
    i1 = jnp.min(jnp.where(l1 == m1, lane, LANES), axis=-1, keepdims=True)
    l2 = jnp.where(lane == i1, neg, l1)
    m2 = jnp.max(l2, axis=-1, keepdims=True)
    i2 = jnp.min(jnp.where(l2 == m2, lane, LANES), axis=-1, keepdims=True)
    e2 = jnp.exp(m2 - m1)
    den = 1.0 + e2
    idx_ref[...] = jnp.where(lane == 0, i1, jnp.where(lane == 1, i2, 0))
    gate_ref[...] = jnp.where(lane == 0, 1.0 / den, jnp.where(lane == 1, e2 / den, 0.0))


def _router(h, router_w):
    t, d = h.shape
    tm = 256
    w = jnp.zeros((d, LANES), F32).at[:, :N_EXPERTS].set(router_w)
    out = pl.BlockSpec((tm, LANES), lambda i: (i, 0))
    return pl.pallas_call(
        _router_kernel,
        grid=(t // tm,),
        in_specs=[pl.BlockSpec((tm, d), lambda i: (i, 0)), pl.BlockSpec((d, LANES), lambda i: (0, 0))],
        out_specs=[out, out],
        out_shape=[jax.ShapeDtypeStruct((t, LANES), jnp.int32), jax.ShapeDtypeStruct((t, LANES), F32)],
        compiler_params=_params("arbitrary"),
        name="router",
    )(h, w)


def _row_copy(src_hbm, row, dst, slot, sem):
    return pltpu.make_async_copy(src_hbm.at[pl.ds(row, 1), :], dst.at[pl.ds(slot, 1), :], sem)


def _gather_kernel(idx_ref, src_hbm, o_ref, buf, sem):
    rows = buf.shape[0]
    base = pl.program_id(0) * rows

    def start(r, carry):
        _row_copy(src_hbm, idx_ref[base + r], buf, r, sem).start()
        return carry

    def wait(r, carry):
        _row_copy(src_hbm, 0, buf, r, sem).wait()
        return carry

    lax.fori_loop(0, rows, start, 0)
    lax.fori_loop(0, rows, wait, 0)
    o_ref[...] = buf[...].astype(o_ref.dtype)


def _gather_rows(src, idx, rows):
    n = idx.shape[0]
    d = src.shape[1]
    grid_spec = pltpu.PrefetchScalarGridSpec(
        num_scalar_prefetch=1,
        grid=(n // rows,),
        in_specs=[pl.BlockSpec(memory_space=pl.ANY)],
        out_specs=pl.BlockSpec((rows, d), lambda i, idx: (i, 0)),
        scratch_shapes=[pltpu.VMEM((rows, d), src.dtype), pltpu.SemaphoreType.DMA(())],
    )
    return pl.pallas_call(
        _gather_kernel,
        grid_spec=grid_spec,
        out_shape=jax.ShapeDtypeStruct((n, d), BF16),
        compiler_params=_params("arbitrary"),
        name="moe_gather",
    )(idx, src)


def _expert_swiglu_kernel(te_ref, tv_ref, x_ref, w1_ref, w3_ref, o_ref):
    i = pl.program_id(1)

    @pl.when(tv_ref[i] == 1)
    def _():
        _mm_swiglu_kernel(x_ref, w1_ref, w3_ref, o_ref)

    @pl.when(tv_ref[i] == 0)
    def _():
        o_ref[...] = jnp.zeros_like(o_ref)


def _expert_swiglu(x, w1, w3, tile_expert, tile_valid, tm, tn):
    cap, k = x.shape
    f = w1.shape[2]
    wspec = pl.BlockSpec((None, k, tn), lambda j, i, te, tv: (te[i], 0, j))
    grid_spec = pltpu.PrefetchScalarGridSpec(
        num_scalar_prefetch=2,
        grid=(f // tn, cap // tm),
        in_specs=[pl.BlockSpec((tm, k), lambda j, i, te, tv: (i, 0)), wspec, wspec],
        out_specs=pl.BlockSpec((tm, tn), lambda j, i, te, tv: (i, j)),
    )
    return pl.pallas_call(
        _expert_swiglu_kernel,
        grid_spec=grid_spec,
        out_shape=jax.ShapeDtypeStruct((cap, f), BF16),
        compiler_params=_params("arbitrary", "arbitrary"),
        name="expert_swiglu",
    )(tile_expert, tile_valid, x, w1, w3)


def _expert_down_kernel(te_ref, tv_ref, x_ref, w_ref, o_ref):
    i = pl.program_id(0)

    @pl.when(tv_ref[i] == 1)
    def _():
        _mm_kernel(x_ref, w_ref, o_ref)

    @pl.when(tv_ref[i] == 0)
    def _():
        o_ref[...] = jnp.zeros_like(o_ref)


def _expert_down(x, w2, tile_expert, tile_valid, tm, tn):
    cap, f = x.shape
    d = w2.shape[2]
    grid_spec = pltpu.PrefetchScalarGridSpec(
        num_scalar_prefetch=2,
        grid=(cap // tm, d // tn),
        in_specs=[
            pl.BlockSpec((tm, f), lambda i, j, te, tv: (i, 0)),
            pl.BlockSpec((None, f, tn), lambda i, j, te, tv: (te[i], 0, j)),
        ],
        out_specs=pl.BlockSpec((tm, tn), lambda i, j, te, tv: (i, j)),
    )
    return pl.pallas_call(
        _expert_down_kernel,
        grid_spec=grid_spec,
        out_shape=jax.ShapeDtypeStruct((cap, d), F32),
        compiler_params=_params("arbitrary", "arbitrary"),
        name="expert_down",
    )(tile_expert, tile_valid, x, w2)


def _combine_kernel(pos_ref, eo_hbm, gate_ref, x_ref, g_ref, fn_ref, o_ref, buf, sem):
    rows = x_ref.shape[0]
    base = pl.program_id(0) * rows * TOP_K

    def start(r, carry):
        for k in range(TOP_K):
            _row_copy(eo_hbm, pos_ref[base + TOP_K * r + k], buf.at[k], r, sem).start()
        return carry

    def wait(r, carry):
        for k in range(TOP_K):
            _row_copy(eo_hbm, 0, buf.at[k], r, sem).wait()
        return carry

    lax.fori_loop(0, rows, start, 0)
    lax.fori_loop(0, rows, wait, 0)
    gate = gate_ref[...]
    y = buf[0] * gate[:, 0:1] + buf[1] * gate[:, 1:2]
    x = x_ref[...] + g_ref[...] * y
    o_ref[...] = x * lax.rsqrt(jnp.mean(x * x, axis=-1, keepdims=True) + EPS) * fn_ref[...]


def _combine(expert_out, pos, gates, x, mods, final_norm, tok, layer):
    t, d = x.shape
    tm = _pick(tok.ctx_len, (256, 128))
    cond = lambda i: tok.cond_of_tile(i, tm)
    grid_spec = pltpu.PrefetchScalarGridSpec(
        num_scalar_prefetch=1,
        grid=(t // tm,),
        in_specs=[
            pl.BlockSpec(memory_space=pl.ANY),
            pl.BlockSpec((tm, LANES), lambda i, pos: (i, 0)),
            pl.BlockSpec((tm, d), lambda i, pos: (i, 0)),
            pl.BlockSpec((None, None, 1, d), lambda i, pos: (layer, cond(i), 0, 5)),
            pl.BlockSpec((1, d), lambda i, pos: (0, 0)),
        ],
        out_specs=pl.BlockSpec((tm, d), lambda i, pos: (i, 0)),
        scratch_shapes=[pltpu.VMEM((TOP_K, tm, d), F32), pltpu.SemaphoreType.DMA(())],
    )
    return pl.pallas_call(
        _combine_kernel,
        grid_spec=grid_spec,
        out_shape=jax.ShapeDtypeStruct((t, d), F32),
        compiler_params=_params("arbitrary"),
        name="moe_combine",
    )(pos, expert_out, gates, x, mods, final_norm)


def _dispatch_plan(idx, n_tok, tm):
    flat_e = idx[:, :TOP_K].reshape(-1)
    n_slots = n_tok * TOP_K
    onehot = (flat_e[:, None] == jnp.arange(N_EXPERTS, dtype=jnp.int32)[None, :]).astype(jnp.int32)
    rank = jnp.sum((jnp.cumsum(onehot, axis=0) - 1) * onehot, axis=1)
    counts = jnp.sum(onehot, axis=0)
    padded = (counts + tm - 1) // tm * tm
    pend = jnp.cumsum(padded)
    pos = ((pend - padded)[flat_e] + rank).astype(jnp.int32)
    n_tiles = n_slots // tm + N_EXPERTS
    src_tok = jnp.zeros((n_tiles * tm,), jnp.int32).at[pos].set(jnp.arange(n_slots, dtype=jnp.int32) // TOP_K)
    tile_start = jnp.arange(n_tiles, dtype=jnp.int32) * tm
    tile_expert = jnp.minimum(jnp.sum(tile_start[:, None] >= pend[None, :], axis=1), N_EXPERTS - 1)
    tile_valid = (tile_start < pend[-1]).astype(jnp.int32)
    return pos, src_tok, tile_expert.astype(jnp.int32), tile_valid


def kernel(x_prompt, x_sample, state_ssm, cache_k, cache_v, c, c_ctx, mod_w, mod_b, norm_w, ssm_in_w, ssm_conv_w, ssm_conv_b, ssm_dt_bias, ssm_a_log, ssm_d, ssm_norm, ssm_out_w, attn_wqkv, attn_q_norm, attn_k_norm, attn_wo, ffn_w1, ffn_w3, ffn_w2, moe_router, moe_w1, moe_w3, moe_w2, final_norm):
    n_ctx_seq, ctx_len, d = x_prompt.shape
    n_lat_seq, lat_len, _ = x_sample.shape
    depth = mod_w.shape[0]
    assert depth == 2 and ssm_in_w.shape[0] == 1 and attn_wqkv.shape[0] == 1
    tok = _Tokens(n_ctx_seq, ctx_len, n_lat_seq, lat_len)
    t = tok.total
    d_inner = ssm_out_w.shape[1]
    ssm_heads = d_inner // SSM_HEAD_DIM
    e_heads = ssm_heads // SSM_GROUPS
    conv_dim = ssm_conv_w.shape[2]
    n_heads = d // HEAD_DIM
    n_kv = n_heads // KV_GROUP
    g, n_state = SSM_GROUPS, SSM_STATE

    x = jnp.concatenate([x_prompt.reshape(tok.n_ctx, d), x_sample.reshape(tok.n_lat, d)], axis=0)
    cond = jnp.zeros((COND_ROWS, d), F32).at[:n_lat_seq].set(c).at[n_lat_seq].set(c_ctx)
    mods = _modulation(cond, mod_w, mod_b)
    norm_w4 = norm_w.reshape(depth * 2, 1, d)

    h = _norm_mod(x, norm_w4, mods, tok, 0, 0, BF16)
    in_w = ssm_in_w[0].astype(BF16)
    zx_cols = d_inner + conv_dim
    tn = _pick(zx_cols, (512, 256, 128))
    proj = _matmul(h, in_w, zx_cols, 0, F32, _pick(t, (1024, 512, 256)), tn, "in_proj")
    dt_w = 2 * ssm_heads
    dt_raw = _matmul(h, in_w, dt_w, zx_cols, F32, _pick(t, (1024, 512, 256)), dt_w, "in_proj_dt")
    xbc = _conv_silu(proj, d_inner, ssm_conv_w[0], ssm_conv_b, tok)
    dt, cs = _dt_prep(dt_raw, ssm_dt_bias.reshape(1, dt_w), ssm_a_log.reshape(1, dt_w))
    grouped = lambda a: a.reshape(t, 2, g, e_heads).transpose(1, 2, 0, 3)
    dtg, csg = grouped(dt), grouped(cs)
    cstg = csg.transpose(0, 1, 3, 2)
    h0 = state_ssm[:, 0].reshape(n_lat_seq, 2, g, e_heads, SSM_HEAD_DIM, n_state)
    h0 = h0.transpose(0, 1, 2, 5, 3, 4).reshape(n_lat_seq, 2, g, n_state, e_heads * SSM_HEAD_DIM)
    y_f, y_b, hfin = _ssd(xbc, dtg, csg, cstg, h0, tok, d_inner)
    d_row = jnp.repeat(ssm_d[0], SSM_HEAD_DIM).reshape(1, d_inner)
    y = _gate_norm(y_f, y_b, xbc, proj, d_row, ssm_norm.reshape(1, d_inner))
    tm = _pick(t, (512, 256))
    x = _matmul_resid(y, ssm_out_w[0].astype(BF16), x, mods, tok, 0, 2, tm, _pick(d, (512, 256, 128)), "out_proj")
    new_state = hfin[:n_ctx_seq].reshape(n_ctx_seq, 2, g, n_state, e_heads, SSM_HEAD_DIM)
    new_state = new_state.transpose(0, 1, 2, 4, 5, 3).reshape(n_ctx_seq, 1, 2, ssm_heads, SSM_HEAD_DIM, n_state)

    h = _norm_mod(x, norm_w4, mods, tok, 0, 1, BF16)
    d_ff = ffn_w1.shape[2]
    u = _matmul_swiglu(h, ffn_w1[0].astype(BF16), ffn_w3[0].astype(BF16), tm, _pick(d_ff, (256, 128)), "ffn_up")
    x = _matmul_resid(u, ffn_w2[0].astype(BF16), x, mods, tok, 0, 5, tm, _pick(d, (256, 128)), "ffn_down")

    h = _norm_mod(x, norm_w4, mods, tok, 1, 0, BF16)
    qkv_w = attn_wqkv.shape[2]
    qkv = _matmul(h, attn_wqkv[0].astype(BF16), qkv_w, 0, F32, _pick(t, (1024, 512, 256)),
                  _pick(qkv_w, (512, 256, 128)), "qkv_proj")
    cos, sin = _rope_tables(lat_len)
    k_col0 = n_heads * HEAD_DIM
    v_col0 = k_col0 + n_kv * HEAD_DIM
    (q,) = _qk_norm_rope(qkv, 0, n_heads, attn_q_norm, cos, sin, tok, False)
    k, k_normed = _qk_norm_rope(qkv, k_col0, n_kv, attn_k_norm, cos, sin, tok, True)
    past = cache_k.shape[2]
    o_ctx = _attention(q, k, qkv, v_col0, 0, n_ctx_seq, ctx_len, n_kv)
    o_lat = _attention(q, k, qkv, v_col0, tok.n_ctx, n_lat_seq, lat_len, n_kv,
                       cache_k[:, 0].reshape(n_lat_seq, past, n_kv * HEAD_DIM),
                       cache_v[:, 0].reshape(n_lat_seq, past, n_kv * HEAD_DIM))
    o = jnp.concatenate([o_ctx, o_lat], axis=0)
    x = _matmul_resid(o, attn_wo[0].astype(BF16), x, mods, tok, 1, 2, tm, _pick(d, (512, 256, 128)), "attn_out")
    new_k = k_normed[:tok.n_ctx].reshape(n_ctx_seq, 1, ctx_len, n_kv, HEAD_DIM)
    new_v = qkv[:tok.n_ctx, v_col0:].reshape(n_ctx_seq, 1, ctx_len, n_kv, HEAD_DIM)

    h = _norm_mod(x, norm_w4, mods, tok, 1, 1, F32)
    idx, gates = _router(h, moe_router[0])
    tmg = 512
    pos, src_tok, tile_expert, tile_valid = _dispatch_plan(idx, t, tmg)
    xs = _gather_rows(h, src_tok, tmg)
    d_fe = moe_w1.shape[3]
    u = _expert_swiglu(xs, moe_w1[0].astype(BF16), moe_w3[0].astype(BF16), tile_expert, tile_valid,
                       tmg, _pick(d_fe, (512, 256, 128)))
    eo = _expert_down(u, moe_w2[0].astype(BF16), tile_expert, tile_valid, tmg, _pick(d, (256, 128)))
    y = _combine(eo, pos, gates, x, mods, final_norm.reshape(1, d), tok, 1)

    y_prompt = y[:tok.n_ctx].reshape(n_ctx_seq, ctx_len, d)
    y_sample = y[tok.n_ctx:].reshape(n_lat_seq, lat_len, d)
    return (y_prompt, y_sample, new_state, new_k, new_v)
```

```python
import functools

import numpy as np
import jax
import jax.numpy as jnp
from jax import lax
from jax.experimental import pallas as pl
from jax.experimental.pallas import tpu as pltpu

F32 = jnp.float32
BF16 = jnp.bfloat16

EPS = 1e-6
LOG2_E = 1.4426950408889634
GRID_W = 64
ROPE_THETA = 10000.0
HEAD_DIM = 128
KV_GROUP = 4
SSM_HEAD_DIM = 64
SSM_GROUPS = 8
SSM_STATE = 128
SSD_CHUNK = 128
N_EXPERTS = 8
TOP_K = 2
LANES = 128
COND_ROWS = 16
VMEM_LIMIT_BYTES = 56 * 1024 * 1024


def _params(*sem):
    return pltpu.CompilerParams(dimension_semantics=sem, vmem_limit_bytes=VMEM_LIMIT_BYTES)


def _silu(x):
    return x * jax.nn.sigmoid(x)


def _bdot(a, b):
    return jnp.dot(a.astype(BF16), b.astype(BF16), preferred_element_type=F32)


def _pick(n, prefs):
    for t in prefs:
        if n % t == 0:
            return t
    return n


class _Tokens:
    def __init__(self, n_ctx_seq, ctx_len, n_lat_seq, lat_len):
        self.n_ctx_seq, self.ctx_len = n_ctx_seq, ctx_len
        self.n_lat_seq, self.lat_len = n_lat_seq, lat_len
        self.n_ctx = n_ctx_seq * ctx_len
        self.n_lat = n_lat_seq * lat_len
        self.total = self.n_ctx + self.n_lat
        assert self.n_ctx % lat_len == 0 and lat_len % ctx_len == 0
        assert ctx_len & (ctx_len - 1) == 0 and lat_len & (lat_len - 1) == 0

    def cond_of_tile(self, i, tm):
        assert self.ctx_len % tm == 0 or tm % self.ctx_len == 0
        assert self.lat_len % tm == 0 and self.n_ctx % tm == 0
        row = i * tm
        return jnp.where(row < self.n_ctx, self.n_lat_seq, (row - self.n_ctx) // self.lat_len)


def _mod_kernel(c_ref, w_ref, b_ref, o_ref):
    o_ref[...] = _bdot(_silu(c_ref[...]), w_ref[...]) + b_ref[...]


def _modulation(cond, mod_w, mod_b):
    depth, d, n = mod_w.shape
    tn = _pick(n, (512, 256, 128))
    out = pl.pallas_call(
        _mod_kernel,
        grid=(depth, n // tn),
        in_specs=[
            pl.BlockSpec((COND_ROWS, d), lambda l, j: (0, 0)),
            pl.BlockSpec((None, d, tn), lambda l, j: (l, 0, j)),
            pl.BlockSpec((None, 1, tn), lambda l, j: (l, 0, j)),
        ],
        out_specs=pl.BlockSpec((None, COND_ROWS, tn), lambda l, j: (l, 0, j)),
        out_shape=jax.ShapeDtypeStruct((depth, COND_ROWS, n), F32),
        compiler_params=_params("arbitrary", "arbitrary"),
        name="modulation",
    )(cond, mod_w, mod_b.reshape(depth, 1, n))
    return out.reshape(depth, COND_ROWS, 1, n)


def _norm_mod_kernel(x_ref, g_ref, sh_ref, sc_ref, o_ref):
    x = x_ref[...]
    y = x * lax.rsqrt(jnp.mean(x * x, axis=-1, keepdims=True) + EPS) * g_ref[...]
    o_ref[...] = (y * (1.0 + sc_ref[...]) + sh_ref[...]).astype(o_ref.dtype)


def _norm_mod(x, norm_w4, mods, tok, layer, which, out_dtype):
    t, d = x.shape
    tm = _pick(tok.ctx_len, (256, 128))
    cond = lambda i: tok.cond_of_tile(i, tm)
    return pl.pallas_call(
        _norm_mod_kernel,
        grid=(t // tm,),
        in_specs=[
            pl.BlockSpec((tm, d), lambda i: (i, 0)),
            pl.BlockSpec((None, 1, d), lambda i: (2 * layer + which, 0, 0)),
            pl.BlockSpec((None, None, 1, d), lambda i: (layer, cond(i), 0, 3 * which)),
            pl.BlockSpec((None, None, 1, d), lambda i: (layer, cond(i), 0, 3 * which + 1)),
        ],
        out_specs=pl.BlockSpec((tm, d), lambda i: (i, 0)),
        out_shape=jax.ShapeDtypeStruct((t, d), out_dtype),
        compiler_params=_params("arbitrary"),
        name="norm_mod",
    )(x, norm_w4, mods, mods)


def _epilogue(kind, acc, extra_refs, o_ref):
    if kind == "swiglu":
        out = _silu(acc[0]) * acc[1]
    elif kind == "resid":
        r_ref, g_ref = extra_refs
        out = r_ref[...] + g_ref[...] * acc[0]
    else:
        out = acc[0]
    o_ref[...] = out.astype(o_ref.dtype)


def _ws_body(first, kind, x_ref, w_refs, extra_refs, o_ref, wb_refs):
    @pl.when(first)
    def _():
        for w_ref, wb_ref in zip(w_refs, wb_refs):
            wb_ref[...] = w_ref[...].astype(BF16)

    x = x_ref[...]
    _epilogue(kind, [jnp.dot(x, wb[...], preferred_element_type=F32) for wb in wb_refs], extra_refs, o_ref)


def _ws_kernel(*refs, n_w, kind):
    n_extra = 2 if kind == "resid" else 0
    x_ref, w_refs = refs[0], refs[1:1 + n_w]
    extra_refs = refs[1 + n_w:1 + n_w + n_extra]
    o_ref = refs[1 + n_w + n_extra]
    _ws_body(pl.program_id(1) == 0, kind, x_ref, w_refs, extra_refs, o_ref, refs[2 + n_w + n_extra:])


def _matmul_ws(x, ws, *, tm, tn, kind, out_dtype, name, n_cols=None, col0=0, res=None, gate_spec=None, mods=None):
    t, k = x.shape
    n_cols = ws[0].shape[1] if n_cols is None else n_cols
    assert t % tm == 0 and n_cols % tn == 0 and col0 % tn == 0
    cb = col0 // tn
    in_specs = [pl.BlockSpec((tm, k), lambda j, i: (i, 0))]
    in_specs += [pl.BlockSpec((k, tn), lambda j, i: (0, cb + j))] * len(ws)
    args = [x, *ws]
    if kind == "resid":
        in_specs += [pl.BlockSpec((tm, tn), lambda j, i: (i, j)), gate_spec]
        args += [res, mods]
    return pl.pallas_call(
        functools.partial(_ws_kernel, n_w=len(ws), kind=kind),
        grid=(n_cols // tn, t // tm),
        in_specs=in_specs,
        out_specs=pl.BlockSpec((tm, tn), lambda j, i: (i, j)),
        out_shape=jax.ShapeDtypeStruct((t, n_cols), out_dtype),
        scratch_shapes=[pltpu.VMEM((k, tn), BF16)] * len(ws),
        compiler_params=_params("arbitrary", "arbitrary"),
        name=name,
    )(*args)


def _gate_spec(tok, layer, chunk, tm, tn, n, weights_outer):
    nb = n // tn
    if weights_outer:
        return pl.BlockSpec((None, None, 1, tn), lambda j, i: (layer, tok.cond_of_tile(i, tm), 0, chunk * nb + j))
    return pl.BlockSpec((None, None, 1, tn), lambda i, j: (layer, tok.cond_of_tile(i, tm), 0, chunk * nb + j))


def _mm_kernel(x_ref, w_ref, o_ref):
    _epilogue("plain", [_bdot(x_ref[...], w_ref[...])], (), o_ref)


def _mm_resid_kernel(x_ref, w_ref, r_ref, g_ref, o_ref):
    _epilogue("resid", [_bdot(x_ref[...], w_ref[...])], (r_ref, g_ref), o_ref)


def _matmul_resid(x, w, res, mods, tok, layer, chunk, tm, tn, name):
    t, k = x.shape
    n = w.shape[1]
    assert t % tm == 0 and n % tn == 0
    return pl.pallas_call(
        _mm_resid_kernel,
        grid=(t // tm, n // tn),
        in_specs=[
            pl.BlockSpec((tm, k), lambda i, j: (i, 0)),
            pl.BlockSpec((k, tn), lambda i, j: (0, j)),
            pl.BlockSpec((tm, tn), lambda i, j: (i, j)),
            _gate_spec(tok, layer, chunk, tm, tn, n, False),
        ],
        out_specs=pl.BlockSpec((tm, tn), lambda i, j: (i, j)),
        out_shape=jax.ShapeDtypeStruct((t, n), F32),
        compiler_params=_params("arbitrary", "arbitrary"),
        name=name,
    )(x, w, res, mods)


def _conv_kernel(x_ref, w_ref, b_ref, o_ref, *, tok):
    i = pl.program_id(0)
    x = x_ref[...]
    rows = x.shape[0]
    seglen = jnp.where(i * rows < tok.n_ctx, tok.ctx_len, tok.lat_len)
    t = lax.broadcasted_iota(jnp.int32, x.shape, 0) & (seglen - 1)
    prev = jnp.where(t == 0, 0.0, pltpu.roll(x, 1, 0))
    nxt = jnp.where(t == seglen - 1, 0.0, pltpu.roll(x, rows - 1, 0))
    w = w_ref[...]
    y = prev * w[0:1] + x * w[1:2] + nxt * w[2:3] + b_ref[...]
    o_ref[...] = _silu(y)


def _conv_silu(proj, col0, conv_w, conv_b, tok):
    t = proj.shape[0]
    n = conv_w.shape[1]
    rows = tok.lat_len
    tc = _pick(n, (512, 256, 128))
    assert col0 % tc == 0
    cb = col0 // tc
    return pl.pallas_call(
        functools.partial(_conv_kernel, tok=tok),
        grid=(t // rows, n // tc),
        in_specs=[
            pl.BlockSpec((rows, tc), lambda i, j: (i, cb + j)),
            pl.BlockSpec((3, tc), lambda i, j: (0, j)),
            pl.BlockSpec((1, tc), lambda i, j: (0, j)),
        ],
        out_specs=pl.BlockSpec((rows, tc), lambda i, j: (i, j)),
        out_shape=jax.ShapeDtypeStruct((t, n), F32),
        compiler_params=_params("arbitrary", "arbitrary"),
        name="conv_silu",
    )(proj, conv_w, conv_b)


def _split3_dot(mat, v):
    hi = v.astype(BF16)
    r1 = v - hi.astype(F32)
    mid = r1.astype(BF16)
    lo = (r1 - mid.astype(F32)).astype(BF16)
    dot = lambda b: jnp.dot(mat, b, preferred_element_type=F32)
    return dot(hi) + dot(mid) + dot(lo)


def _dt_kernel(raw_ref, bias_ref, alog_ref, dt_ref, cs_ref, *, heads):
    x = raw_ref[...] + bias_ref[...]
    dt = jnp.maximum(x, 0.0) + jnp.log1p(jnp.exp(-jnp.abs(x)))
    la = dt * (-jnp.exp(alog_ref[...])) * LOG2_E
    r = lax.broadcasted_iota(jnp.int32, (SSD_CHUNK, SSD_CHUNK), 0)
    c = lax.broadcasted_iota(jnp.int32, (SSD_CHUNK, SSD_CHUNK), 1)
    lower = jnp.where(r >= c, 1.0, 0.0).astype(BF16)
    upper = jnp.where(r <= c, 1.0, 0.0).astype(BF16)
    dt_ref[...] = dt
    cs_ref[:, :heads] = _split3_dot(lower, la[:, :heads])
    cs_ref[:, heads:] = _split3_dot(upper, la[:, heads:])


def _dt_prep(dt_raw, dt_bias, a_log):
    t, w = dt_raw.shape
    spec = pl.BlockSpec((SSD_CHUNK, w), lambda i: (i, 0))
    par = pl.BlockSpec((1, w), lambda i: (0, 0))
    return pl.pallas_call(
        functools.partial(_dt_kernel, heads=w // 2),
        grid=(t // SSD_CHUNK,),
        in_specs=[spec, par, par],
        out_specs=[spec, spec],
        out_shape=[jax.ShapeDtypeStruct((t, w), F32)] * 2,
        compiler_params=_params("arbitrary"),
        name="dt_prep",
    )(dt_raw, dt_bias, a_log)


def _expand_matrix(heads, width):
    r = jnp.arange(3 * heads, dtype=jnp.int32)[:, None] % heads
    j = jnp.arange(heads * width, dtype=jnp.int32)[None, :] // width
    return (r == j).astype(BF16)


def _expand_lanes(v, sel_ref):
    hi = v.astype(BF16)
    r1 = v - hi.astype(F32)
    mid = r1.astype(BF16)
    lo = (r1 - mid.astype(F32)).astype(BF16)
    return jnp.dot(jnp.concatenate([hi, mid, lo], axis=1), sel_ref[...], preferred_element_type=F32)


def _ssd_direction(d, x_ref, b_ref, c_ref, dt_ref, cs_ref, cst_ref, selp_ref, selc_ref, y_ref, h_scr, s_scr, heads):
    p = SSM_HEAD_DIM
    bmat = b_ref[...]
    cmat = c_ref[...].astype(BF16)
    cb = lax.dot_general(cmat, bmat.astype(BF16), (((1,), (1,)), ((), ())), preferred_element_type=F32)
    r = lax.broadcasted_iota(jnp.int32, (SSD_CHUNK, SSD_CHUNK), 0)
    c = lax.broadcasted_iota(jnp.int32, (SSD_CHUNK, SSD_CHUNK), 1)
    mask = (r >= c) if d == 0 else (r <= c)
    end = SSD_CHUNK - 1 if d == 0 else 0
    assert heads % 2 == 0 and 2 * p == SSD_CHUNK
    cs = cs_ref[...]
    cst = cst_ref[...]
    xs = x_ref[...] * _expand_lanes(dt_ref[...], selp_ref)
    col = _expand_lanes(cs, selc_ref)
    y_off = jnp.dot(cmat, h_scr[d].astype(BF16), preferred_element_type=F32)
    low = c < p
    for e in range(0, heads, 2):
        lanes = slice(e * p, (e + 2) * p)
        xs_pair = xs[:, lanes]
        col_a = col[:, e * SSD_CHUNK:(e + 1) * SSD_CHUNK]
        col_b = col[:, (e + 1) * SSD_CHUNK:(e + 2) * SSD_CHUNK]
        y_diag = None
        for col_e, row_e, keep in ((col_a, cst[e:e + 1, :], low), (col_b, cst[e + 1:e + 2, :], ~low)):
            scores = cb * jnp.exp2(jnp.where(mask, col_e - row_e, -jnp.inf))
            part = jnp.dot(scores.astype(BF16), jnp.where(keep, xs_pair, 0.0).astype(BF16),
                           preferred_element_type=F32)
            y_diag = part if y_diag is None else y_diag + part
        cs_pair = jnp.where(low, col_a, col_b)
        y_ref[:, lanes] = y_diag + y_off[:, lanes] * jnp.exp2(cs_pair)
        s_scr[:, lanes] = (xs_pair * jnp.exp2(cs_pair[end:end + 1, :] - cs_pair)).astype(BF16)
    new = jnp.dot(bmat.T.astype(BF16), s_scr[...], preferred_element_type=F32)
    h_scr[d] = h_scr[d] * _expand_lanes(jnp.exp2(cs[end:end + 1, :]), selp_ref) + new


def _ssd_kernel(fr_ref, br_ref, first_ref, last_ref, seq_ref, lat_ref,
                xf_ref, bf_ref, cf_ref, dtf_ref, csf_ref, cstf_ref,
                xb_ref, bb_ref, cb_ref, dtb_ref, csb_ref, cstb_ref,
                h0_ref, selp_ref, selc_ref, yf_ref, yb_ref, hfin_ref, h_scr, s_scr, *, heads):
    s = pl.program_id(1)

    @pl.when(first_ref[s] == 1)
    def _():
        h_scr[...] = jnp.where(lat_ref[s] == 1, h0_ref[...], 0.0)

    _ssd_direction(0, xf_ref, bf_ref, cf_ref, dtf_ref, csf_ref, cstf_ref, selp_ref, selc_ref, yf_ref, h_scr, s_scr, heads)
    _ssd_direction(1, xb_ref, bb_ref, cb_ref, dtb_ref, csb_ref, cstb_ref, selp_ref, selc_ref, yb_ref, h_scr, s_scr, heads)

    @pl.when(last_ref[s] == 1)
    def _():
        hfin_ref[...] = h_scr[...]


def _ssd_steps(tok):
    rows = []
    chunk = 0
    for n_seq, length, lat in ((tok.n_ctx_seq, tok.ctx_len, 0), (tok.n_lat_seq, tok.lat_len, 1)):
        nc = length // SSD_CHUNK
        for b in range(n_seq):
            for c in range(nc):
                slot = b if lat else b
                rows.append((chunk + c, chunk + nc - 1 - c, int(c == 0), int(c == nc - 1), slot, lat))
            chunk += nc
    return [jnp.asarray(np.array(col, np.int32)) for col in zip(*rows)]


def _ssd(xbc, dtg, csg, cstg, h0, tok, d_inner):
    t = xbc.shape[0]
    g, n = SSM_GROUPS, SSM_STATE
    gw = d_inner // g
    heads = gw // SSM_HEAD_DIM
    assert n == SSD_CHUNK and gw % n == 0
    b0 = d_inner // n
    c0 = b0 + g
    steps = _ssd_steps(tok)
    n_steps = steps[0].shape[0]
    n_fin = tok.n_ctx_seq + 1

    def spec(shape, fn):
        return pl.BlockSpec(shape, fn)

    def dir_specs(d):
        row = (lambda s, fr, br: fr[s]) if d == 0 else (lambda s, fr, br: br[s])
        return [
            spec((SSD_CHUNK, gw), lambda gi, s, fr, br, *_: (row(s, fr, br), gi)),
            spec((SSD_CHUNK, n), lambda gi, s, fr, br, *_: (row(s, fr, br), b0 + gi)),
            spec((SSD_CHUNK, n), lambda gi, s, fr, br, *_: (row(s, fr, br), c0 + gi)),
            spec((None, None, SSD_CHUNK, heads), lambda gi, s, fr, br, *_: (d, gi, row(s, fr, br), 0)),
            spec((None, None, SSD_CHUNK, heads), lambda gi, s, fr, br, *_: (d, gi, row(s, fr, br), 0)),
            spec((None, None, heads, SSD_CHUNK), lambda gi, s, fr, br, *_: (d, gi, 0, row(s, fr, br))),
        ]

    def h0_map(gi, s, fr, br, first, last, seq, lat):
        return (jnp.where(lat[s] == 1, seq[s], 0), 0, gi, 0, 0)

    def hfin_map(gi, s, fr, br, first, last, seq, lat):
        return (jnp.where(lat[s] == 1, n_fin - 1, seq[s]), 0, gi, 0, 0)

    state_block = (None, 2, None, n, gw)
    sel_p = _expand_matrix(heads, SSM_HEAD_DIM)
    sel_c = _expand_matrix(heads, SSD_CHUNK)
    const = lambda a: spec(a.shape, lambda gi, s, *_: (0, 0))
    grid_spec = pltpu.PrefetchScalarGridSpec(
        num_scalar_prefetch=6,
        grid=(g, n_steps),
        in_specs=dir_specs(0) + dir_specs(1) + [spec(state_block, h0_map), const(sel_p), const(sel_c)],
        out_specs=[
            spec((SSD_CHUNK, gw), lambda gi, s, fr, br, *_: (fr[s], gi)),
            spec((SSD_CHUNK, gw), lambda gi, s, fr, br, *_: (br[s], gi)),
            spec(state_block, hfin_map),
        ],
        scratch_shapes=[pltpu.VMEM((2, n, gw), F32), pltpu.VMEM((SSD_CHUNK, gw), BF16)],
    )
    return pl.pallas_call(
        functools.partial(_ssd_kernel, heads=heads),
        grid_spec=grid_spec,
        out_shape=[
            jax.ShapeDtypeStruct((t, d_inner), F32),
            jax.ShapeDtypeStruct((t, d_inner), F32),
            jax.ShapeDtypeStruct((n_fin, 2, g, n, gw), F32),
        ],
        compiler_params=_params("arbitrary", "arbitrary"),
        name="ssd_scan",
    )(*steps, xbc, xbc, xbc, dtg, csg, cstg, xbc, xbc, xbc, dtg, csg, cstg, h0, sel_p, sel_c)


def _gate_norm_kernel(yf_ref, yb_ref, xv_ref, z_ref, d_ref, g_ref, o_ref):
    y = yf_ref[...] + yb_ref[...] + d_ref[...] * xv_ref[...]
    y = y * _silu(z_ref[...])
    y = y * lax.rsqrt(jnp.mean(y * y, axis=-1, keepdims=True) + EPS) * g_ref[...]
    o_ref[...] = y.astype(o_ref.dtype)


def _gate_norm(y_f, y_b, xbc, proj, d_row, norm_row):
    t, d_inner = y_f.shape
    gw = d_inner // SSM_GROUPS
    tm = 256
    blk = pl.BlockSpec((tm, gw), lambda i, j: (i, j))
    par = pl.BlockSpec((1, gw), lambda i, j: (0, j))
    return pl.pallas_call(
        _gate_norm_kernel,
        grid=(t // tm, SSM_GROUPS),
        in_specs=[blk, blk, blk, blk, par, par],
        out_specs=blk,
        out_shape=jax.ShapeDtypeStruct((t, d_inner), BF16),
        compiler_params=_params("arbitrary", "arbitrary"),
        name="gate_norm",
    )(y_f, y_b, xbc, proj, d_row, norm_row)


def _qk_norm_rope_kernel(x_ref, g_ref, cos_ref, sin_ref, o_ref, *rest, n_ctx_tiles):
    i = pl.program_id(0)
    x = x_ref[...]
    y = x * lax.rsqrt(jnp.mean(x * x, axis=-1, keepdims=True) + EPS) * g_ref[...]
    if rest:
        rest[0][...] = y
    lane = lax.broadcasted_iota(jnp.int32, y.shape, 1)
    low = (lane & (HEAD_DIM // 2 - 1)) < HEAD_DIM // 4
    partner = jnp.where(low, pltpu.roll(y, HEAD_DIM - HEAD_DIM // 4, 1), pltpu.roll(y, HEAD_DIM // 4, 1))
    rot = y * cos_ref[...] + partner * sin_ref[...]
    o_ref[...] = jnp.where(i >= n_ctx_tiles, rot, y).astype(o_ref.dtype)


def _qk_norm_rope(qkv, col0, n_heads, gain, cos, sin, tok, emit_f32):
    t = qkv.shape[0]
    rows = tok.lat_len
    hb = col0 // HEAD_DIM
    blk_in = pl.BlockSpec((rows, HEAD_DIM), lambda i, j: (i, hb + j))
    blk = pl.BlockSpec((rows, HEAD_DIM), lambda i, j: (i, j))
    par = pl.BlockSpec((1, HEAD_DIM), lambda i, j: (0, 0))
    tab = pl.BlockSpec((rows, HEAD_DIM), lambda i, j: (0, 0))
    shapes = [jax.ShapeDtypeStruct((t, n_heads * HEAD_DIM), BF16)]
    if emit_f32:
        shapes.append(jax.ShapeDtypeStruct((t, n_heads * HEAD_DIM), F32))
    return pl.pallas_call(
        functools.partial(_qk_norm_rope_kernel, n_ctx_tiles=tok.n_ctx // rows),
        grid=(t // rows, n_heads),
        in_specs=[blk_in, par, tab, tab],
        out_specs=[blk] * len(shapes),
        out_shape=shapes,
        compiler_params=_params("arbitrary", "arbitrary"),
        name="qk_norm_rope",
    )(qkv, gain, cos, sin)


def _attn_kernel(q_ref, k_ref, v_ref, *rest, has_cache):
    if has_cache:
        kc_ref, vc_ref, o_ref = rest
    else:
        (o_ref,) = rest
    scale = HEAD_DIM ** -0.5
    nt = (((1,), (1,)), ((), ()))
    k = k_ref[...]
    v = v_ref[...].astype(BF16)
    if has_cache:
        kc = kc_ref[...].astype(BF16)
        vc = vc_ref[...].astype(BF16)
    for h in range(KV_GROUP):
        lanes = slice(h * HEAD_DIM, (h + 1) * HEAD_DIM)
        q = q_ref[:, lanes]
        s = lax.dot_general(q, k, nt, preferred_element_type=F32) * scale
        m = jnp.max(s, axis=-1, keepdims=True)
        if has_cache:
            sc = lax.dot_general(q, kc, nt, preferred_element_type=F32) * scale
            m = jnp.maximum(m, jnp.max(sc, axis=-1, keepdims=True))
            ec = jnp.exp(sc - m)
        e = jnp.exp(s - m)
        den = jnp.sum(e, axis=-1, keepdims=True)
        if has_cache:
            den = den + jnp.sum(ec, axis=-1, keepdims=True)
        inv = 1.0 / den
        acc = jnp.dot((e * inv).astype(BF16), v, preferred_element_type=F32)
        if has_cache:
            acc = acc + jnp.dot((ec * inv).astype(BF16), vc, preferred_element_type=F32)
        o_ref[:, lanes] = acc.astype(o_ref.dtype)


def _attention(q, k, qkv, v_col0, row0, n_seq, seq_len, n_kv, cache_k=None, cache_v=None):
    tq = _pick(seq_len, (256, 128))
    nq = seq_len // tq
    sb = row0 // seq_len
    vb = v_col0 // HEAD_DIM
    qw = KV_GROUP * HEAD_DIM
    in_specs = [
        pl.BlockSpec((tq, qw), lambda b, h, i: ((sb + b) * nq + i, h)),
        pl.BlockSpec((seq_len, HEAD_DIM), lambda b, h, i: (sb + b, h)),
        pl.BlockSpec((seq_len, HEAD_DIM), lambda b, h, i: (sb + b, vb + h)),
    ]
    args = [q, k, qkv]
    if cache_k is not None:
        past = cache_k.shape[1]
        cspec = pl.BlockSpec((None, past, HEAD_DIM), lambda b, h, i: (b, 0, h))
        in_specs += [cspec, cspec]
        args += [cache_k, cache_v]
    return pl.pallas_call(
        functools.partial(_attn_kernel, has_cache=cache_k is not None),
        grid=(n_seq, n_kv, nq),
        in_specs=in_specs,
        out_specs=pl.BlockSpec((tq, qw), lambda b, h, i: (b * nq + i, h)),
        out_shape=jax.ShapeDtypeStruct((n_seq * seq_len, n_kv * qw), BF16),
        compiler_params=_params("arbitrary", "arbitrary", "arbitrary"),
        name="attention",
    )(*args)


def _rope_tables(n_tokens):
    t = jnp.arange(n_tokens)
    row = (t // GRID_W).astype(F32)
    col = (t % GRID_W).astype(F32)
    axis_dim = HEAD_DIM // 2
    inv_freq = ROPE_THETA ** (-jnp.arange(0, axis_dim, 2, dtype=F32) / axis_dim)
    ang = jnp.concatenate([row[:, None] * inv_freq] * 2 + [col[:, None] * inv_freq] * 2, axis=-1)
    sign = jnp.tile(jnp.repeat(jnp.array([-1.0, 1.0], F32), HEAD_DIM // 4), 2)
    return jnp.cos(ang), jnp.sin(ang) * sign


def _router_kernel(x_ref, w_ref, idx_ref, gate_ref):
    logits = _bdot(x_ref[...], w_ref[...])
    lane = lax.broadcasted_iota(jnp.int32, logits.shape, 1)
    neg = -jnp.inf
    l1 = jnp.where(lane < N_EXPERTS, logits, neg)
    m1 = jnp.max(l1, axis=-1, keepdims=True)
    i1 = jnp.min(jnp.where(l1 == m1, lane, LANES), axis=-1, keepdims=True)
    l2 = jnp.where(lane == i1, neg, l1)
    m2 = jnp.max(l2, axis=-1, keepdims=True)
    i2 = jnp.min(jnp.where(l2 == m2, lane, LANES), axis=-1, keepdims=True)
    e2 = jnp.exp(m2 - m1)
    den = 1.0 + e2
    idx_ref[...] = jnp.where(lane == 0, i1, jnp.where(lane == 1, i2, 0))
    gate_ref[...] = jnp.where(lane == 0, 1.0 / den, jnp.where(lane == 1, e2 / den, 0.0))


def _router(h, router_w):
    t, d = h.shape
    tm = 256
    w = jnp.zeros((d, LANES), F32).at[:, :N_EXPERTS].set(router_w)
    out = pl.BlockSpec((tm, LANES), lambda i: (i, 0))
    return pl.pallas_call(
        _router_kernel,
        grid=(t // tm,),
        in_specs=[pl.BlockSpec((tm, d), lambda i: (i, 0)), pl.BlockSpec((d, LANES), lambda i: (0, 0))],
        out_specs=[out, out],
        out_shape=[jax.ShapeDtypeStruct((t, LANES), jnp.int32), jax.ShapeDtypeStruct((t, LANES), F32)],
        compiler_params=_params("arbitrary"),
        name="router",
    )(h, w)


def _row_copy(src_hbm, row, dst, slot, sem):
    return pltpu.make_async_copy(src_hbm.at[pl.ds(row, 1), :], dst.at[pl.ds(slot, 1), :], sem)


def _gather_kernel(idx_ref, src_hbm, o_ref, buf, sem):
    rows = buf.shape[0]
    base = pl.program_id(0) * rows

    def start(r, carry):
        _row_copy(src_hbm, idx_ref[base + r], buf, r, sem).start()
        return carry

    def wait(r, carry):
        _row_copy(src_hbm, 0, buf, r, sem).wait()
        return carry

    lax.fori_loop(0, rows, start, 0)
    lax.fori_loop(0, rows, wait, 0)
    o_ref[...] = buf[...].astype(o_ref.dtype)


def _gather_rows(src, idx, rows):
    n = idx.shape[0]
    d = src.shape[1]
    grid_spec = pltpu.PrefetchScalarGridSpec(
        num_scalar_prefetch=1,
        grid=(n // rows,),
        in_specs=[pl.BlockSpec(memory_space=pl.ANY)],
        out_specs=pl.BlockSpec((rows, d), lambda i, idx: (i, 0)),
        scratch_shapes=[pltpu.VMEM((rows, d), src.dtype), pltpu.SemaphoreType.DMA(())],
    )
    return pl.pallas_call(
        _gather_kernel,
        grid_spec=grid_spec,
        out_shape=jax.ShapeDtypeStruct((n, d), BF16),
        compiler_params=_params("arbitrary"),
        name="moe_gather",
    )(idx, src)


def _expert_swiglu_kernel(te_ref, tv_ref, tf_ref, x_ref, w1_ref, w3_ref, o_ref, wb1_ref, wb3_ref):
    i = pl.program_id(1)

    @pl.when(tv_ref[i] == 1)
    def _():
        _ws_body(tf_ref[i] == 1, "swiglu", x_ref, (w1_ref, w3_ref), (), o_ref, (wb1_ref, wb3_ref))

    @pl.when(tv_ref[i] == 0)
    def _():
        o_ref[...] = jnp.zeros_like(o_ref)


def _expert_swiglu(x, w1, w3, tile_expert, tile_valid, tile_first, tm, tn):
    cap, k = x.shape
    f = w1.shape[2]
    wspec = pl.BlockSpec((None, k, tn), lambda j, i, te, tv, tf: (te[i], 0, j))
    grid_spec = pltpu.PrefetchScalarGridSpec(
        num_scalar_prefetch=3,
        grid=(f // tn, cap // tm),
        in_specs=[pl.BlockSpec((tm, k), lambda j, i, te, tv, tf: (i, 0)), wspec, wspec],
        out_specs=pl.BlockSpec((tm, tn), lambda j, i, te, tv, tf: (i, j)),
        scratch_shapes=[pltpu.VMEM((k, tn), BF16)] * 2,
    )
    return pl.pallas_call(
        _expert_swiglu_kernel,
        grid_spec=grid_spec,
        out_shape=jax.ShapeDtypeStruct((cap, f), BF16),
        compiler_params=_params("arbitrary", "arbitrary"),
        name="expert_swiglu",
    )(tile_expert, tile_valid, tile_first, x, w1, w3)


def _expert_down_kernel(te_ref, tv_ref, x_ref, w_ref, o_ref):
    i = pl.program_id(0)

    @pl.when(tv_ref[i] == 1)
    def _():
        _mm_kernel(x_ref, w_ref, o_ref)

    @pl.when(tv_ref[i] == 0)
    def _():
        o_ref[...] = jnp.zeros_like(o_ref)


def _expert_down(x, w2, tile_expert, tile_valid, tm, tn):
    cap, f = x.shape
    d = w2.shape[2]
    grid_spec = pltpu.PrefetchScalarGridSpec(
        num_scalar_prefetch=2,
        grid=(cap // tm, d // tn),
        in_specs=[
            pl.BlockSpec((tm, f), lambda i, j, te, tv: (i, 0)),
            pl.BlockSpec((None, f, tn), lambda i, j, te, tv: (te[i], 0, j)),
        ],
        out_specs=pl.BlockSpec((tm, tn), lambda i, j, te, tv: (i, j)),
    )
    return pl.pallas_call(
        _expert_down_kernel,
        grid_spec=grid_spec,
        out_shape=jax.ShapeDtypeStruct((cap, d), F32),
        compiler_params=_params("arbitrary", "arbitrary"),
        name="expert_down",
    )(tile_expert, tile_valid, x, w2)


def _combine_kernel(pos_ref, eo_hbm, gate_ref, x_ref, g_ref, fn_ref, octx_ref, olat_ref, buf, sem, *, n_ctx_tiles):
    rows = x_ref.shape[0]
    tile = pl.program_id(0)
    base = tile * rows * TOP_K

    def start(r, carry):
        for k in range(TOP_K):
            _row_copy(eo_hbm, pos_ref[base + TOP_K * r + k], buf.at[k], r, sem).start()
        return carry

    def wait(r, carry):
        for k in range(TOP_K):
            _row_copy(eo_hbm, 0, buf.at[k], r, sem).wait()
        return carry

    lax.fori_loop(0, rows, start, 0)
    lax.fori_loop(0, rows, wait, 0)
    gate = gate_ref[...]
    y = buf[0] * gate[:, 0:1] + buf[1] * gate[:, 1:2]
    x = x_ref[...] + g_ref[...] * y
    out = x * lax.rsqrt(jnp.mean(x * x, axis=-1, keepdims=True) + EPS) * fn_ref[...]

    @pl.when(tile < n_ctx_tiles)
    def _():
        octx_ref[...] = out

    @pl.when(tile >= n_ctx_tiles)
    def _():
        olat_ref[...] = out


def _combine(expert_out, pos, gates, x, mods, final_norm, tok, layer):
    t, d = x.shape
    tm = _pick(tok.ctx_len, (256, 128))
    cond = lambda i: tok.cond_of_tile(i, tm)
    n_ctx_tiles = tok.n_ctx // tm
    grid_spec = pltpu.PrefetchScalarGridSpec(
        num_scalar_prefetch=1,
        grid=(t // tm,),
        in_specs=[
            pl.BlockSpec(memory_space=pl.ANY),
            pl.BlockSpec((tm, LANES), lambda i, pos: (i, 0)),
            pl.BlockSpec((tm, d), lambda i, pos: (i, 0)),
            pl.BlockSpec((None, None, 1, d), lambda i, pos: (layer, cond(i), 0, 5)),
            pl.BlockSpec((1, d), lambda i, pos: (0, 0)),
        ],
        out_specs=[
            pl.BlockSpec((tm, d), lambda i, pos: (jnp.minimum(i, n_ctx_tiles - 1), 0)),
            pl.BlockSpec((tm, d), lambda i, pos: (jnp.maximum(i - n_ctx_tiles, 0), 0)),
        ],
        scratch_shapes=[pltpu.VMEM((TOP_K, tm, d), F32), pltpu.SemaphoreType.DMA(())],
    )
    return pl.pallas_call(
        functools.partial(_combine_kernel, n_ctx_tiles=n_ctx_tiles),
        grid_spec=grid_spec,
        out_shape=[jax.ShapeDtypeStruct((tok.n_ctx, d), F32), jax.ShapeDtypeStruct((tok.n_lat, d), F32)],
        compiler_params=_params("arbitrary"),
        name="moe_combine",
    )(pos, expert_out, gates, x, mods, final_norm)


def _dispatch_plan(idx, n_tok, tm):
    flat_e = idx[:, :TOP_K].reshape(-1)
    n_slots = n_tok * TOP_K
    onehot = (flat_e[:, None] == jnp.arange(N_EXPERTS, dtype=jnp.int32)[None, :]).astype(jnp.int32)
    rank = jnp.sum((jnp.cumsum(onehot, axis=0) - 1) * onehot, axis=1)
    counts = jnp.sum(onehot, axis=0)
    padded = (counts + tm - 1) // tm * tm
    pend = jnp.cumsum(padded)
    pos = ((pend - padded)[flat_e] + rank).astype(jnp.int32)
    n_tiles = n_slots // tm + N_EXPERTS
    src_tok = jnp.zeros((n_tiles * tm,), jnp.int32).at[pos].set(jnp.arange(n_slots, dtype=jnp.int32) // TOP_K)
    tile_start = jnp.arange(n_tiles, dtype=jnp.int32) * tm
    tile_expert = jnp.minimum(jnp.sum(tile_start[:, None] >= pend[None, :], axis=1), N_EXPERTS - 1)
    tile_expert = tile_expert.astype(jnp.int32)
    tile_valid = (tile_start < pend[-1]).astype(jnp.int32)
    tile_first = jnp.concatenate([jnp.ones((1,), jnp.int32), (tile_expert[1:] != tile_expert[:-1]).astype(jnp.int32)])
    return pos, src_tok, tile_expert, tile_valid, tile_first


def kernel(x_prompt, x_sample, state_ssm, cache_k, cache_v, c, c_ctx, mod_w, mod_b, norm_w, ssm_in_w, ssm_conv_w, ssm_conv_b, ssm_dt_bias, ssm_a_log, ssm_d, ssm_norm, ssm_out_w, attn_wqkv, attn_q_norm, attn_k_norm, attn_wo, ffn_w1, ffn_w3, ffn_w2, moe_router, moe_w1, moe_w3, moe_w2, final_norm):
    n_ctx_seq, ctx_len, d = x_prompt.shape
    n_lat_seq, lat_len, _ = x_sample.shape
    depth = mod_w.shape[0]
    assert depth == 2 and ssm_in_w.shape[0] == 1 and attn_wqkv.shape[0] == 1
    tok = _Tokens(n_ctx_seq, ctx_len, n_lat_seq, lat_len)
    t = tok.total
    d_inner = ssm_out_w.shape[1]
    ssm_heads = d_inner // SSM_HEAD_DIM
    e_heads = ssm_heads // SSM_GROUPS
    conv_dim = ssm_conv_w.shape[2]
    n_heads = d // HEAD_DIM
    n_kv = n_heads // KV_GROUP
    g, n_state = SSM_GROUPS, SSM_STATE

    x = jnp.concatenate([x_prompt.reshape(tok.n_ctx, d), x_sample.reshape(tok.n_lat, d)], axis=0)
    cond = jnp.zeros((COND_ROWS, d), F32).at[:n_lat_seq].set(c).at[n_lat_seq].set(c_ctx)
    mods = _modulation(cond, mod_w, mod_b)
    norm_w4 = norm_w.reshape(depth * 2, 1, d)

    h = _norm_mod(x, norm_w4, mods, tok, 0, 0, BF16)
    zx_cols = d_inner + conv_dim
    tm_big = _pick(lat_len, (1024, 512, 256))
    proj = _matmul_ws(h, [ssm_in_w[0]], tm=tm_big, tn=_pick(zx_cols, (512, 256, 128)), kind="plain",
                      out_dtype=F32, name="in_proj", n_cols=zx_cols)
    dt_w = 2 * ssm_heads
    dt_raw = _matmul_ws(h, [ssm_in_w[0]], tm=tm_big, tn=dt_w, kind="plain", out_dtype=F32, name="in_proj_dt",
                        n_cols=dt_w, col0=zx_cols)
    xbc = _conv_silu(proj, d_inner, ssm_conv_w[0], ssm_conv_b, tok)
    dt, cs = _dt_prep(dt_raw, ssm_dt_bias.reshape(1, dt_w), ssm_a_log.reshape(1, dt_w))
    grouped = lambda a: a.reshape(t, 2, g, e_heads).transpose(1, 2, 0, 3)
    dtg, csg = grouped(dt), grouped(cs)
    cstg = csg.transpose(0, 1, 3, 2)
    h0 = state_ssm[:, 0].reshape(n_lat_seq, 2, g, e_heads, SSM_HEAD_DIM, n_state)
    h0 = h0.transpose(0, 1, 2, 5, 3, 4).reshape(n_lat_seq, 2, g, n_state, e_heads * SSM_HEAD_DIM)
    y_f, y_b, hfin = _ssd(xbc, dtg, csg, cstg, h0, tok, d_inner)
    d_row = jnp.repeat(ssm_d[0], SSM_HEAD_DIM).reshape(1, d_inner)
    y = _gate_norm(y_f, y_b, xbc, proj, d_row, ssm_norm.reshape(1, d_inner))
    tm = _pick(t, (512, 256))
    x = _matmul_resid(y, ssm_out_w[0].astype(BF16), x, mods, tok, 0, 2, tm, _pick(d, (512, 256, 128)), "out_proj")
    new_state = hfin[:n_ctx_seq].reshape(n_ctx_seq, 2, g, n_state, e_heads, SSM_HEAD_DIM)
    new_state = new_state.transpose(0, 1, 2, 4, 5, 3).reshape(n_ctx_seq, 1, 2, ssm_heads, SSM_HEAD_DIM, n_state)

    h = _norm_mod(x, norm_w4, mods, tok, 0, 1, BF16)
    d_ff = ffn_w1.shape[2]
    u = _matmul_ws(h, [ffn_w1[0], ffn_w3[0]], tm=tm_big, tn=_pick(d_ff, (256, 128)), kind="swiglu", out_dtype=BF16,
                   name="ffn_up")
    x = _matmul_resid(u, ffn_w2[0].astype(BF16), x, mods, tok, 0, 5, tm, _pick(d, (512, 256, 128)), "ffn_down")

    h = _norm_mod(x, norm_w4, mods, tok, 1, 0, BF16)
    qkv_w = attn_wqkv.shape[2]
    qkv = _matmul_ws(h, [attn_wqkv[0]], tm=tm_big, tn=_pick(qkv_w, (512, 256, 128)), kind="plain", out_dtype=F32,
                     name="qkv_proj")
    cos, sin = _rope_tables(lat_len)
    k_col0 = n_heads * HEAD_DIM
    v_col0 = k_col0 + n_kv * HEAD_DIM
    (q,) = _qk_norm_rope(qkv, 0, n_heads, attn_q_norm, cos, sin, tok, False)
    k, k_normed = _qk_norm_rope(qkv, k_col0, n_kv, attn_k_norm, cos, sin, tok, True)
    past = cache_k.shape[2]
    o_ctx = _attention(q, k, qkv, v_col0, 0, n_ctx_seq, ctx_len, n_kv)
    o_lat = _attention(q, k, qkv, v_col0, tok.n_ctx, n_lat_seq, lat_len, n_kv,
                       cache_k[:, 0].reshape(n_lat_seq, past, n_kv * HEAD_DIM),
                       cache_v[:, 0].reshape(n_lat_seq, past, n_kv * HEAD_DIM))
    o = jnp.concatenate([o_ctx, o_lat], axis=0)
    tn = _pick(d, (512, 256, 128))
    x = _matmul_ws(o, [attn_wo[0]], tm=tm_big, tn=tn, kind="resid", out_dtype=F32, name="attn_out", res=x,
                   gate_spec=_gate_spec(tok, 1, 2, tm_big, tn, d, True), mods=mods)
    new_k = k_normed[:tok.n_ctx].reshape(n_ctx_seq, 1, ctx_len, n_kv, HEAD_DIM)
    new_v = qkv[:tok.n_ctx, v_col0:].reshape(n_ctx_seq, 1, ctx_len, n_kv, HEAD_DIM)

    h = _norm_mod(x, norm_w4, mods, tok, 1, 1, F32)
    idx, gates = _router(h, moe_router[0])
    tmg = 512
    pos, src_tok, tile_expert, tile_valid, tile_first = _dispatch_plan(idx, t, tmg)
    xs = _gather_rows(h, src_tok, tmg)
    d_fe = moe_w1.shape[3]
    u = _expert_swiglu(xs, moe_w1[0], moe_w3[0], tile_expert, tile_valid, tile_first,
                       tmg, _pick(d_fe, (512, 256, 128)))
    eo = _expert_down(u, moe_w2[0].astype(BF16), tile_expert, tile_valid, tmg, _pick(d, (256, 128)))
    y_ctx, y_lat = _combine(eo, pos, gates, x, mods, final_norm.reshape(1, d), tok, 1)

    y_prompt = y_ctx.reshape(n_ctx_seq, ctx_len, d)
    y_sample = y_lat.reshape(n_lat_seq, lat_len, d)
    return (y_prompt, y_sample, new_state, new_k, new_v)
```

```python
import functools

import numpy as np
import jax
import jax.numpy as jnp
from jax import lax
from jax.experimental import pallas as pl
from jax.experimental.pallas import tpu as pltpu

F32 = jnp.float32
BF16 = jnp.bfloat16

EPS = 1e-6
LOG2_E = 1.4426950408889634
GRID_W = 64
ROPE_THETA = 10000.0
HEAD_DIM = 128
KV_GROUP = 4
SSM_HEAD_DIM = 64
SSM_GROUPS = 8
SSM_STATE = 128
SSD_CHUNK = 128
N_EXPERTS = 8
TOP_K = 2
LANES = 128
COND_ROWS = 16
DMA_UNROLL = 8
VMEM_LIMIT_BYTES = 56 * 1024 * 1024


def _params(*sem):
    return pltpu.CompilerParams(dimension_semantics=sem, vmem_limit_bytes=VMEM_LIMIT_BYTES)


def _silu(x):
    return x * jax.nn.sigmoid(x)


def _bdot(a, b):
    return jnp.dot(a.astype(BF16), b.astype(BF16), preferred_element_type=F32)


def _pick(n, prefs):
    for t in prefs:
        if n % t == 0:
            return t
    return n


class _Tokens:
    def __init__(self, n_ctx_seq, ctx_len, n_lat_seq, lat_len):
        self.n_ctx_seq, self.ctx_len = n_ctx_seq, ctx_len
        self.n_lat_seq, self.lat_len = n_lat_seq, lat_len
        self.n_ctx = n_ctx_seq * ctx_len
        self.n_lat = n_lat_seq * lat_len
        self.total = self.n_ctx + self.n_lat
        assert self.n_ctx % lat_len == 0 and lat_len % ctx_len == 0
        assert ctx_len & (ctx_len - 1) == 0 and lat_len & (lat_len - 1) == 0

    def cond_of_tile(self, i, tm):
        assert self.ctx_len % tm == 0 or tm % self.ctx_len == 0
        assert self.lat_len % tm == 0 and self.n_ctx % tm == 0
        row = i * tm
        return jnp.where(row < self.n_ctx, self.n_lat_seq, (row - self.n_ctx) // self.lat_len)


def _mod_kernel(c_ref, w_ref, b_ref, o_ref):
    o_ref[...] = _bdot(_silu(c_ref[...]), w_ref[...]) + b_ref[...]


def _modulation(cond, mod_w, mod_b):
    depth, d, n = mod_w.shape
    tn = _pick(n, (512, 256, 128))
    out = pl.pallas_call(
        _mod_kernel,
        grid=(depth, n // tn),
        in_specs=[
            pl.BlockSpec((COND_ROWS, d), lambda l, j: (0, 0)),
            pl.BlockSpec((None, d, tn), lambda l, j: (l, 0, j)),
            pl.BlockSpec((None, 1, tn), lambda l, j: (l, 0, j)),
        ],
        out_specs=pl.BlockSpec((None, COND_ROWS, tn), lambda l, j: (l, 0, j)),
        out_shape=jax.ShapeDtypeStruct((depth, COND_ROWS, n), F32),
        compiler_params=_params("arbitrary", "arbitrary"),
        name="modulation",
    )(cond, mod_w, mod_b.reshape(depth, 1, n))
    return out.reshape(depth, COND_ROWS, 1, n)


def _norm_mod_kernel(x_ref, g_ref, sh_ref, sc_ref, o_ref):
    x = x_ref[...]
    y = x * lax.rsqrt(jnp.mean(x * x, axis=-1, keepdims=True) + EPS) * g_ref[...]
    o_ref[...] = (y * (1.0 + sc_ref[...]) + sh_ref[...]).astype(o_ref.dtype)


def _norm_mod(x, norm_w4, mods, tok, layer, which, out_dtype):
    t, d = x.shape
    tm = _pick(tok.ctx_len, (256, 128))
    cond = lambda i: tok.cond_of_tile(i, tm)
    return pl.pallas_call(
        _norm_mod_kernel,
        grid=(t // tm,),
        in_specs=[
            pl.BlockSpec((tm, d), lambda i: (i, 0)),
            pl.BlockSpec((None, 1, d), lambda i: (2 * layer + which, 0, 0)),
            pl.BlockSpec((None, None, 1, d), lambda i: (layer, cond(i), 0, 3 * which)),
            pl.BlockSpec((None, None, 1, d), lambda i: (layer, cond(i), 0, 3 * which + 1)),
        ],
        out_specs=pl.BlockSpec((tm, d), lambda i: (i, 0)),
        out_shape=jax.ShapeDtypeStruct((t, d), out_dtype),
        compiler_params=_params("arbitrary"),
        name="norm_mod",
    )(x, norm_w4, mods, mods)


def _head_norm_rope(x, gain, cos, sin, rope):
    y = x * lax.rsqrt(jnp.mean(x * x, axis=-1, keepdims=True) + EPS) * gain
    lane = lax.broadcasted_iota(jnp.int32, y.shape, 1)
    low = (lane & (HEAD_DIM // 2 - 1)) < HEAD_DIM // 4
    partner = jnp.where(low, pltpu.roll(y, HEAD_DIM - HEAD_DIM // 4, 1), pltpu.roll(y, HEAD_DIM // 4, 1))
    return y, jnp.where(rope, y * cos + partner * sin, y)


def _epilogue(kind, acc, extra_refs, o_ref):
    if kind == "swiglu":
        out = _silu(acc[0]) * acc[1]
    elif kind == "resid":
        r_ref, g_ref = extra_refs
        out = r_ref[...] + g_ref[...] * acc[0]
    else:
        out = acc[0]
    o_ref[...] = out.astype(o_ref.dtype)


def _cast_weights(first, w_refs, wb_refs):
    @pl.when(first)
    def _():
        for w_ref, wb_ref in zip(w_refs, wb_refs):
            wb_ref[...] = w_ref[...].astype(BF16)


def _ws_body(first, kind, x_ref, w_refs, extra_refs, o_ref, wb_refs):
    _cast_weights(first, w_refs, wb_refs)
    x = x_ref[...]
    _epilogue(kind, [jnp.dot(x, wb[...], preferred_element_type=F32) for wb in wb_refs], extra_refs, o_ref)


ROW_SPLITS = 4


def _expert_rows_matmul(kind, n_rows, x_ref, w_refs, o_ref):
    tm = x_ref.shape[0]
    q = tm // ROW_SPLITS
    for b in range(1, ROW_SPLITS + 1):
        m = b * q

        @pl.when((n_rows > m - q) & (n_rows <= m))
        def _(m=m):
            x = x_ref[0:m, :]
            acc = [jnp.dot(x, w[...], preferred_element_type=F32) for w in w_refs]
            _epilogue(kind, acc, (), o_ref.at[0:m, :])
            if m < tm:
                o_ref[m:, :] = jnp.zeros((tm - m, o_ref.shape[1]), o_ref.dtype)

    @pl.when(n_rows == 0)
    def _():
        o_ref[...] = jnp.zeros_like(o_ref)


_N_EXTRA = {"plain": 0, "swiglu": 0, "resid": 2}


def _ws_kernel(*refs, n_w, kind):
    n_extra = _N_EXTRA[kind]
    x_ref, w_refs = refs[0], refs[1:1 + n_w]
    extra_refs = refs[1 + n_w:1 + n_w + n_extra]
    o_ref = refs[1 + n_w + n_extra]
    _ws_body(pl.program_id(1) == 0, kind, x_ref, w_refs, extra_refs, o_ref, refs[2 + n_w + n_extra:])


def _matmul_ws(x, ws, *, tm, tn, kind, out_dtype, name, n_cols=None, col0=0, extra=(), extra_specs=()):
    t, k = x.shape
    n_cols = ws[0].shape[1] if n_cols is None else n_cols
    assert t % tm == 0 and n_cols % tn == 0 and col0 % tn == 0 and len(extra) == _N_EXTRA[kind]
    cb = col0 // tn
    in_specs = [pl.BlockSpec((tm, k), lambda j, i: (i, 0))]
    in_specs += [pl.BlockSpec((k, tn), lambda j, i: (0, cb + j))] * len(ws)
    in_specs += list(extra_specs)
    args = [x, *ws, *extra]
    return pl.pallas_call(
        functools.partial(_ws_kernel, n_w=len(ws), kind=kind),
        grid=(n_cols // tn, t // tm),
        in_specs=in_specs,
        out_specs=pl.BlockSpec((tm, tn), lambda j, i: (i, j)),
        out_shape=jax.ShapeDtypeStruct((t, n_cols), out_dtype),
        scratch_shapes=[pltpu.VMEM((k, tn), BF16)] * len(ws),
        compiler_params=_params("arbitrary", "arbitrary"),
        name=name,
    )(*args)


def _gate_spec(tok, layer, chunk, tm, tn, n, weights_outer):
    nb = n // tn
    if weights_outer:
        return pl.BlockSpec((None, None, 1, tn), lambda j, i: (layer, tok.cond_of_tile(i, tm), 0, chunk * nb + j))
    return pl.BlockSpec((None, None, 1, tn), lambda i, j: (layer, tok.cond_of_tile(i, tm), 0, chunk * nb + j))


def _mm_resid_kernel(x_ref, w_ref, r_ref, g_ref, o_ref):
    _epilogue("resid", [_bdot(x_ref[...], w_ref[...])], (r_ref, g_ref), o_ref)


def _matmul_resid(x, w, res, mods, tok, layer, chunk, tm, tn, name):
    t, k = x.shape
    n = w.shape[1]
    assert t % tm == 0 and n % tn == 0
    return pl.pallas_call(
        _mm_resid_kernel,
        grid=(t // tm, n // tn),
        in_specs=[
            pl.BlockSpec((tm, k), lambda i, j: (i, 0)),
            pl.BlockSpec((k, tn), lambda i, j: (0, j)),
            pl.BlockSpec((tm, tn), lambda i, j: (i, j)),
            _gate_spec(tok, layer, chunk, tm, tn, n, False),
        ],
        out_specs=pl.BlockSpec((tm, tn), lambda i, j: (i, j)),
        out_shape=jax.ShapeDtypeStruct((t, n), F32),
        compiler_params=_params("arbitrary", "arbitrary"),
        name=name,
    )(x, w, res, mods)


def _conv_kernel(x_ref, w_ref, b_ref, o_ref, *, tok):
    i = pl.program_id(0)
    x = x_ref[...]
    rows = x.shape[0]
    seglen = jnp.where(i * rows < tok.n_ctx, tok.ctx_len, tok.lat_len)
    t = lax.broadcasted_iota(jnp.int32, x.shape, 0) & (seglen - 1)
    prev = jnp.where(t == 0, 0.0, pltpu.roll(x, 1, 0))
    nxt = jnp.where(t == seglen - 1, 0.0, pltpu.roll(x, rows - 1, 0))
    w = w_ref[...]
    y = prev * w[0:1] + x * w[1:2] + nxt * w[2:3] + b_ref[...]
    o_ref[...] = _silu(y)


def _conv_silu(proj, col0, conv_w, conv_b, tok):
    t = proj.shape[0]
    n = conv_w.shape[1]
    rows = tok.lat_len
    tc = _pick(n, (512, 256, 128))
    assert col0 % tc == 0
    cb = col0 // tc
    return pl.pallas_call(
        functools.partial(_conv_kernel, tok=tok),
        grid=(t // rows, n // tc),
        in_specs=[
            pl.BlockSpec((rows, tc), lambda i, j: (i, cb + j)),
            pl.BlockSpec((3, tc), lambda i, j: (0, j)),
            pl.BlockSpec((1, tc), lambda i, j: (0, j)),
        ],
        out_specs=pl.BlockSpec((rows, tc), lambda i, j: (i, j)),
        out_shape=jax.ShapeDtypeStruct((t, n), F32),
        compiler_params=_params("arbitrary", "arbitrary"),
        name="conv_silu",
    )(proj, conv_w, conv_b)


def _split3_dot(mat, v):
    hi = v.astype(BF16)
    r1 = v - hi.astype(F32)
    mid = r1.astype(BF16)
    lo = (r1 - mid.astype(F32)).astype(BF16)
    dot = lambda b: jnp.dot(mat, b, preferred_element_type=F32)
    return dot(hi) + dot(mid) + dot(lo)


def _dt_kernel(raw_ref, bias_ref, alog_ref, dt_ref, cs_ref, *, heads):
    x = raw_ref[...] + bias_ref[...]
    dt = jnp.maximum(x, 0.0) + jnp.log1p(jnp.exp(-jnp.abs(x)))
    la = dt * (-jnp.exp(alog_ref[...])) * LOG2_E
    r = lax.broadcasted_iota(jnp.int32, (SSD_CHUNK, SSD_CHUNK), 0)
    c = lax.broadcasted_iota(jnp.int32, (SSD_CHUNK, SSD_CHUNK), 1)
    lower = jnp.where(r >= c, 1.0, 0.0).astype(BF16)
    upper = jnp.where(r <= c, 1.0, 0.0).astype(BF16)
    dt_ref[...] = dt
    cs_ref[:, :heads] = _split3_dot(lower, la[:, :heads])
    cs_ref[:, heads:] = _split3_dot(upper, la[:, heads:])


def _dt_prep(dt_raw, dt_bias, a_log):
    t, w = dt_raw.shape
    spec = pl.BlockSpec((SSD_CHUNK, w), lambda i: (i, 0))
    par = pl.BlockSpec((1, w), lambda i: (0, 0))
    return pl.pallas_call(
        functools.partial(_dt_kernel, heads=w // 2),
        grid=(t // SSD_CHUNK,),
        in_specs=[spec, par, par],
        out_specs=[spec, spec],
        out_shape=[jax.ShapeDtypeStruct((t, w), F32)] * 2,
        compiler_params=_params("arbitrary"),
        name="dt_prep",
    )(dt_raw, dt_bias, a_log)


def _expand_matrix(heads, width):
    r = jnp.arange(3 * heads, dtype=jnp.int32)[:, None] % heads
    j = jnp.arange(heads * width, dtype=jnp.int32)[None, :] // width
    return (r == j).astype(BF16)


def _expand_lanes(v, sel_ref):
    hi = v.astype(BF16)
    r1 = v - hi.astype(F32)
    mid = r1.astype(BF16)
    lo = (r1 - mid.astype(F32)).astype(BF16)
    return jnp.dot(jnp.concatenate([hi, mid, lo], axis=1), sel_ref[...], preferred_element_type=F32)


def _ssd_direction(d, x_ref, b_ref, c_ref, dt_ref, cs_ref, cst_ref, selp_ref, selc_ref, y_ref, h_scr, s_scr, heads):
    p = SSM_HEAD_DIM
    bmat = b_ref[...]
    cmat = c_ref[...].astype(BF16)
    cb = lax.dot_general(cmat, bmat.astype(BF16), (((1,), (1,)), ((), ())), preferred_element_type=F32)
    r = lax.broadcasted_iota(jnp.int32, (SSD_CHUNK, SSD_CHUNK), 0)
    c = lax.broadcasted_iota(jnp.int32, (SSD_CHUNK, SSD_CHUNK), 1)
    mask = (r >= c) if d == 0 else (r <= c)
    end = SSD_CHUNK - 1 if d == 0 else 0
    assert heads % 2 == 0 and 2 * p == SSD_CHUNK
    cs = cs_ref[...]
    cst = cst_ref[...]
    xs = x_ref[...] * _expand_lanes(dt_ref[...], selp_ref)
    col = _expand_lanes(cs, selc_ref)
    y_off = jnp.dot(cmat, h_scr[d].astype(BF16), preferred_element_type=F32)
    low = c < p
    for e in range(0, heads, 2):
        lanes = slice(e * p, (e + 2) * p)
        xs_pair = xs[:, lanes]
        col_a = col[:, e * SSD_CHUNK:(e + 1) * SSD_CHUNK]
        col_b = col[:, (e + 1) * SSD_CHUNK:(e + 2) * SSD_CHUNK]
        y_diag = None
        for col_e, row_e, keep in ((col_a, cst[e:e + 1, :], low), (col_b, cst[e + 1:e + 2, :], ~low)):
            scores = cb * jnp.exp2(jnp.where(mask, col_e - row_e, -jnp.inf))
            part = jnp.dot(scores.astype(BF16), jnp.where(keep, xs_pair, 0.0).astype(BF16),
                           preferred_element_type=F32)
            y_diag = part if y_diag is None else y_diag + part
        cs_pair = jnp.where(low, col_a, col_b)
        y_ref[:, lanes] = y_diag + y_off[:, lanes] * jnp.exp2(cs_pair)
        s_scr[:, lanes] = (xs_pair * jnp.exp2(cs_pair[end:end + 1, :] - cs_pair)).astype(BF16)
    new = jnp.dot(bmat.T.astype(BF16), s_scr[...], preferred_element_type=F32)
    h_scr[d] = h_scr[d] * _expand_lanes(jnp.exp2(cs[end:end + 1, :]), selp_ref) + new


def _ssd_kernel(fr_ref, br_ref, first_ref, last_ref, seq_ref, lat_ref,
                xf_ref, bf_ref, cf_ref, dtf_ref, csf_ref, cstf_ref,
                xb_ref, bb_ref, cb_ref, dtb_ref, csb_ref, cstb_ref,
                h0_ref, selp_ref, selc_ref, yf_ref, yb_ref, hfin_ref, h_scr, s_scr, *, heads):
    s = pl.program_id(1)

    @pl.when(first_ref[s] == 1)
    def _():
        h_scr[...] = jnp.where(lat_ref[s] == 1, h0_ref[...], 0.0)

    _ssd_direction(0, xf_ref, bf_ref, cf_ref, dtf_ref, csf_ref, cstf_ref, selp_ref, selc_ref, yf_ref, h_scr, s_scr, heads)
    _ssd_direction(1, xb_ref, bb_ref, cb_ref, dtb_ref, csb_ref, cstb_ref, selp_ref, selc_ref, yb_ref, h_scr, s_scr, heads)

    @pl.when((last_ref[s] == 1) & (lat_ref[s] == 0))
    def _():
        hfin_ref[...] = h_scr[...]


def _ssd_steps(tok):
    rows = []
    chunk = 0
    for n_seq, length, lat in ((tok.n_ctx_seq, tok.ctx_len, 0), (tok.n_lat_seq, tok.lat_len, 1)):
        nc = length // SSD_CHUNK
        for b in range(n_seq):
            for c in range(nc):
                slot = b if lat else b
                rows.append((chunk + c, chunk + nc - 1 - c, int(c == 0), int(c == nc - 1), slot, lat))
            chunk += nc
    return [jnp.asarray(np.array(col, np.int32)) for col in zip(*rows)]


def _ssd(xbc, dtg, csg, cstg, h0, tok, d_inner):
    t = xbc.shape[0]
    g, n = SSM_GROUPS, SSM_STATE
    gw = d_inner // g
    heads = gw // SSM_HEAD_DIM
    assert n == SSD_CHUNK and gw % n == 0
    b0 = d_inner // n
    c0 = b0 + g
    steps = _ssd_steps(tok)
    n_steps = steps[0].shape[0]
    n_fin = tok.n_ctx_seq

    def spec(shape, fn):
        return pl.BlockSpec(shape, fn)

    def dir_specs(d):
        row = (lambda s, fr, br: fr[s]) if d == 0 else (lambda s, fr, br: br[s])
        return [
            spec((SSD_CHUNK, gw), lambda gi, s, fr, br, *_: (row(s, fr, br), gi)),
            spec((SSD_CHUNK, n), lambda gi, s, fr, br, *_: (row(s, fr, br), b0 + gi)),
            spec((SSD_CHUNK, n), lambda gi, s, fr, br, *_: (row(s, fr, br), c0 + gi)),
            spec((None, None, SSD_CHUNK, heads), lambda gi, s, fr, br, *_: (d, gi, row(s, fr, br), 0)),
            spec((None, None, SSD_CHUNK, heads), lambda gi, s, fr, br, *_: (d, gi, row(s, fr, br), 0)),
            spec((None, None, heads, SSD_CHUNK), lambda gi, s, fr, br, *_: (d, gi, 0, row(s, fr, br))),
        ]

    def h0_map(gi, s, fr, br, first, last, seq, lat):
        return (jnp.where(lat[s] == 1, seq[s], 0), 0, gi, 0, 0)

    def hfin_map(gi, s, fr, br, first, last, seq, lat):
        return (jnp.where(lat[s] == 1, n_fin - 1, seq[s]), 0, gi, 0, 0)

    state_block = (None, 2, None, n, gw)
    sel_p = _expand_matrix(heads, SSM_HEAD_DIM)
    sel_c = _expand_matrix(heads, SSD_CHUNK)
    const = lambda a: spec(a.shape, lambda gi, s, *_: (0, 0))
    grid_spec = pltpu.PrefetchScalarGridSpec(
        num_scalar_prefetch=6,
        grid=(g, n_steps),
        in_specs=dir_specs(0) + dir_specs(1) + [spec(state_block, h0_map), const(sel_p), const(sel_c)],
        out_specs=[
            spec((SSD_CHUNK, gw), lambda gi, s, fr, br, *_: (fr[s], gi)),
            spec((SSD_CHUNK, gw), lambda gi, s, fr, br, *_: (br[s], gi)),
            spec(state_block, hfin_map),
        ],
        scratch_shapes=[pltpu.VMEM((2, n, gw), F32), pltpu.VMEM((SSD_CHUNK, gw), BF16)],
    )
    return pl.pallas_call(
        functools.partial(_ssd_kernel, heads=heads),
        grid_spec=grid_spec,
        out_shape=[
            jax.ShapeDtypeStruct((t, d_inner), F32),
            jax.ShapeDtypeStruct((t, d_inner), F32),
            jax.ShapeDtypeStruct((n_fin, 2, g, n, gw), F32),
        ],
        compiler_params=_params("arbitrary", "arbitrary"),
        name="ssd_scan",
    )(*steps, xbc, xbc, xbc, dtg, csg, cstg, xbc, xbc, xbc, dtg, csg, cstg, h0, sel_p, sel_c)


def _gate_norm_kernel(yf_ref, yb_ref, xv_ref, z_ref, d_ref, g_ref, o_ref):
    y = yf_ref[...] + yb_ref[...] + d_ref[...] * xv_ref[...]
    y = y * _silu(z_ref[...])
    y = y * lax.rsqrt(jnp.mean(y * y, axis=-1, keepdims=True) + EPS) * g_ref[...]
    o_ref[...] = y.astype(o_ref.dtype)


def _gate_norm(y_f, y_b, xbc, proj, d_row, norm_row):
    t, d_inner = y_f.shape
    gw = d_inner // SSM_GROUPS
    tm = 256
    blk = pl.BlockSpec((tm, gw), lambda i, j: (i, j))
    par = pl.BlockSpec((1, gw), lambda i, j: (0, j))
    return pl.pallas_call(
        _gate_norm_kernel,
        grid=(t // tm, SSM_GROUPS),
        in_specs=[blk, blk, blk, blk, par, par],
        out_specs=blk,
        out_shape=jax.ShapeDtypeStruct((t, d_inner), BF16),
        compiler_params=_params("arbitrary", "arbitrary"),
        name="gate_norm",
    )(y_f, y_b, xbc, proj, d_row, norm_row)


def _qk_norm_rope_kernel(x_ref, g_ref, cos_ref, sin_ref, o_ref, *rest, n_ctx_tiles):
    rope = pl.program_id(0) >= n_ctx_tiles
    y, out = _head_norm_rope(x_ref[...], g_ref[...], cos_ref[...], sin_ref[...], rope)
    if rest:
        rest[0][...] = y
    o_ref[...] = out.astype(o_ref.dtype)


def _qk_norm_rope(qkv, col0, n_heads, gain, cos, sin, tok, emit_f32):
    t = qkv.shape[0]
    rows = tok.lat_len
    hb = col0 // HEAD_DIM
    blk_in = pl.BlockSpec((rows, HEAD_DIM), lambda i, j: (i, hb + j))
    blk = pl.BlockSpec((rows, HEAD_DIM), lambda i, j: (i, j))
    par = pl.BlockSpec((1, HEAD_DIM), lambda i, j: (0, 0))
    tab = pl.BlockSpec((rows, HEAD_DIM), lambda i, j: (0, 0))
    shapes = [jax.ShapeDtypeStruct((t, n_heads * HEAD_DIM), BF16)]
    if emit_f32:
        shapes.append(jax.ShapeDtypeStruct((t, n_heads * HEAD_DIM), F32))
    return pl.pallas_call(
        functools.partial(_qk_norm_rope_kernel, n_ctx_tiles=tok.n_ctx // rows),
        grid=(t // rows, n_heads),
        in_specs=[blk_in, par, tab, tab],
        out_specs=[blk] * len(shapes),
        out_shape=shapes,
        compiler_params=_params("arbitrary", "arbitrary"),
        name="qk_norm_rope",
    )(qkv, gain, cos, sin)


def _attn_kernel(q_ref, k_ref, v_ref, *rest, has_cache):
    if has_cache:
        kc_ref, vc_ref, o_ref = rest
    else:
        (o_ref,) = rest
    c = HEAD_DIM ** -0.5 * LOG2_E
    nt = (((1,), (1,)), ((), ()))
    k = k_ref[...]
    v = v_ref[...].astype(BF16)
    if has_cache:
        kc = kc_ref[...].astype(BF16)
        vc = vc_ref[...].astype(BF16)
    for h in range(KV_GROUP):
        lanes = slice(h * HEAD_DIM, (h + 1) * HEAD_DIM)
        q = q_ref[:, lanes]
        s = lax.dot_general(q, k, nt, preferred_element_type=F32)
        m = jnp.max(s, axis=-1, keepdims=True)
        if has_cache:
            sc = lax.dot_general(q, kc, nt, preferred_element_type=F32)
            m = jnp.maximum(m, jnp.max(sc, axis=-1, keepdims=True))
            ec = jnp.exp2((sc - m) * c)
        e = jnp.exp2((s - m) * c)
        den = jnp.sum(e, axis=-1, keepdims=True)
        if has_cache:
            den = den + jnp.sum(ec, axis=-1, keepdims=True)
        inv = 1.0 / den
        acc = jnp.dot((e * inv).astype(BF16), v, preferred_element_type=F32)
        if has_cache:
            acc = acc + jnp.dot((ec * inv).astype(BF16), vc, preferred_element_type=F32)
        o_ref[:, lanes] = acc.astype(o_ref.dtype)


def _attention(q, k, qkv, v_col0, row0, n_seq, seq_len, n_kv, cache_k=None, cache_v=None):
    tq = _pick(seq_len, (256, 128))
    nq = seq_len // tq
    sb = row0 // seq_len
    vb = v_col0 // HEAD_DIM
    qw = KV_GROUP * HEAD_DIM
    in_specs = [
        pl.BlockSpec((tq, qw), lambda b, h, i: ((sb + b) * nq + i, h)),
        pl.BlockSpec((seq_len, HEAD_DIM), lambda b, h, i: (sb + b, h)),
        pl.BlockSpec((seq_len, HEAD_DIM), lambda b, h, i: (sb + b, vb + h)),
    ]
    args = [q, k, qkv]
    if cache_k is not None:
        past = cache_k.shape[1]
        cspec = pl.BlockSpec((None, past, HEAD_DIM), lambda b, h, i: (b, 0, h))
        in_specs += [cspec, cspec]
        args += [cache_k, cache_v]
    return pl.pallas_call(
        functools.partial(_attn_kernel, has_cache=cache_k is not None),
        grid=(n_seq, n_kv, nq),
        in_specs=in_specs,
        out_specs=pl.BlockSpec((tq, qw), lambda b, h, i: (b * nq + i, h)),
        out_shape=jax.ShapeDtypeStruct((n_seq * seq_len, n_kv * qw), BF16),
        compiler_params=_params("arbitrary", "arbitrary", "arbitrary"),
        name="attention",
    )(*args)


def _rope_tables(n_tokens):
    t = jnp.arange(n_tokens)
    row = (t // GRID_W).astype(F32)
    col = (t % GRID_W).astype(F32)
    axis_dim = HEAD_DIM // 2
    inv_freq = ROPE_THETA ** (-jnp.arange(0, axis_dim, 2, dtype=F32) / axis_dim)
    ang = jnp.concatenate([row[:, None] * inv_freq] * 2 + [col[:, None] * inv_freq] * 2, axis=-1)
    sign = jnp.tile(jnp.repeat(jnp.array([-1.0, 1.0], F32), HEAD_DIM // 4), 2)
    return jnp.cos(ang), jnp.sin(ang) * sign


def _router_kernel(x_ref, w_ref, idx_ref, gate_ref):
    logits = _bdot(x_ref[...], w_ref[...])
    lane = lax.broadcasted_iota(jnp.int32, logits.shape, 1)
    neg = -jnp.inf
    l1 = jnp.where(lane < N_EXPERTS, logits, neg)
    m1 = jnp.max(l1, axis=-1, keepdims=True)
    i1 = jnp.min(jnp.where(l1 == m1, lane, LANES), axis=-1, keepdims=True)
    l2 = jnp.where(lane == i1, neg, l1)
    m2 = jnp.max(l2, axis=-1, keepdims=True)
    i2 = jnp.min(jnp.where(l2 == m2, lane, LANES), axis=-1, keepdims=True)
    e2 = jnp.exp(m2 - m1)
    den = 1.0 + e2
    idx_ref[...] = jnp.where(lane == 0, i1, jnp.where(lane == 1, i2, 0))
    gate_ref[...] = jnp.where(lane == 0, 1.0 / den, jnp.where(lane == 1, e2 / den, 0.0))


def _router(h, router_w):
    t, d = h.shape
    tm = 256
    w = jnp.zeros((d, LANES), F32).at[:, :N_EXPERTS].set(router_w)
    out = pl.BlockSpec((tm, LANES), lambda i: (i, 0))
    return pl.pallas_call(
        _router_kernel,
        grid=(t // tm,),
        in_specs=[pl.BlockSpec((tm, d), lambda i: (i, 0)), pl.BlockSpec((d, LANES), lambda i: (0, 0))],
        out_specs=[out, out],
        out_shape=[jax.ShapeDtypeStruct((t, LANES), jnp.int32), jax.ShapeDtypeStruct((t, LANES), F32)],
        compiler_params=_params("arbitrary"),
        name="router",
    )(h, w)


def _row_copy(src_hbm, row, dst, slot, sem):
    return pltpu.make_async_copy(src_hbm.at[pl.ds(row, 1), :], dst.at[pl.ds(slot, 1), :], sem)


def _gather_kernel(idx_ref, src_hbm, o_ref, buf, sem):
    rows = buf.shape[0]
    base = pl.program_id(0) * rows

    def start(r, carry):
        _row_copy(src_hbm, idx_ref[base + r], buf, r, sem).start()
        return carry

    def wait(r, carry):
        _row_copy(src_hbm, 0, buf, r, sem).wait()
        return carry

    lax.fori_loop(0, rows, start, 0, unroll=DMA_UNROLL)
    lax.fori_loop(0, rows, wait, 0, unroll=DMA_UNROLL)
    o_ref[...] = buf[...].astype(o_ref.dtype)


def _gather_rows(src, idx, rows):
    n = idx.shape[0]
    d = src.shape[1]
    grid_spec = pltpu.PrefetchScalarGridSpec(
        num_scalar_prefetch=1,
        grid=(n // rows,),
        in_specs=[pl.BlockSpec(memory_space=pl.ANY)],
        out_specs=pl.BlockSpec((rows, d), lambda i, idx: (i, 0)),
        scratch_shapes=[pltpu.VMEM((rows, d), src.dtype), pltpu.SemaphoreType.DMA(())],
    )
    return pl.pallas_call(
        _gather_kernel,
        grid_spec=grid_spec,
        out_shape=jax.ShapeDtypeStruct((n, d), BF16),
        compiler_params=_params("arbitrary"),
        name="moe_gather",
    )(idx, src)


def _expert_swiglu_kernel(te_ref, tr_ref, tf_ref, x_ref, w1_ref, w3_ref, o_ref, wb1_ref, wb3_ref):
    i = pl.program_id(1)
    _cast_weights((tf_ref[i] == 1) & (tr_ref[i] > 0), (w1_ref, w3_ref), (wb1_ref, wb3_ref))
    _expert_rows_matmul("swiglu", tr_ref[i], x_ref, (wb1_ref, wb3_ref), o_ref)


def _expert_swiglu(x, w1, w3, tile_expert, tile_rows, tile_first, tm, tn):
    cap, k = x.shape
    f = w1.shape[2]
    wspec = pl.BlockSpec((None, k, tn), lambda j, i, te, tv, tf: (te[i], 0, j))
    grid_spec = pltpu.PrefetchScalarGridSpec(
        num_scalar_prefetch=3,
        grid=(f // tn, cap // tm),
        in_specs=[pl.BlockSpec((tm, k), lambda j, i, te, tv, tf: (i, 0)), wspec, wspec],
        out_specs=pl.BlockSpec((tm, tn), lambda j, i, te, tv, tf: (i, j)),
        scratch_shapes=[pltpu.VMEM((k, tn), BF16)] * 2,
    )
    return pl.pallas_call(
        _expert_swiglu_kernel,
        grid_spec=grid_spec,
        out_shape=jax.ShapeDtypeStruct((cap, f), BF16),
        compiler_params=_params("arbitrary", "arbitrary"),
        name="expert_swiglu",
    )(tile_expert, tile_rows, tile_first, x, w1, w3)


def _expert_down_kernel(te_ref, tr_ref, x_ref, w_ref, o_ref):
    _expert_rows_matmul("plain", tr_ref[pl.program_id(0)], x_ref, (w_ref,), o_ref)


def _expert_down(x, w2, tile_expert, tile_rows, tm, tn):
    cap, f = x.shape
    d = w2.shape[2]
    grid_spec = pltpu.PrefetchScalarGridSpec(
        num_scalar_prefetch=2,
        grid=(cap // tm, d // tn),
        in_specs=[
            pl.BlockSpec((tm, f), lambda i, j, te, tv: (i, 0)),
            pl.BlockSpec((None, f, tn), lambda i, j, te, tv: (te[i], 0, j)),
        ],
        out_specs=pl.BlockSpec((tm, tn), lambda i, j, te, tv: (i, j)),
    )
    return pl.pallas_call(
        _expert_down_kernel,
        grid_spec=grid_spec,
        out_shape=jax.ShapeDtypeStruct((cap, d), F32),
        compiler_params=_params("arbitrary", "arbitrary"),
        name="expert_down",
    )(tile_expert, tile_rows, x, w2)


def _combine_kernel(pos_ref, eo_hbm, gate_ref, x_ref, g_ref, fn_ref, octx_ref, olat_ref, buf, sem, *, n_ctx_tiles):
    rows = x_ref.shape[0]
    tile = pl.program_id(0)
    base = tile * rows * TOP_K

    def start(r, carry):
        for k in range(TOP_K):
            _row_copy(eo_hbm, pos_ref[base + TOP_K * r + k], buf.at[k], r, sem).start()
        return carry

    def wait(r, carry):
        for k in range(TOP_K):
            _row_copy(eo_hbm, 0, buf.at[k], r, sem).wait()
        return carry

    lax.fori_loop(0, rows, start, 0, unroll=DMA_UNROLL)
    lax.fori_loop(0, rows, wait, 0, unroll=DMA_UNROLL)
    gate = gate_ref[...]
    y = buf[0] * gate[:, 0:1] + buf[1] * gate[:, 1:2]
    x = x_ref[...] + g_ref[...] * y
    out = x * lax.rsqrt(jnp.mean(x * x, axis=-1, keepdims=True) + EPS) * fn_ref[...]

    @pl.when(tile < n_ctx_tiles)
    def _():
        octx_ref[...] = out

    @pl.when(tile >= n_ctx_tiles)
    def _():
        olat_ref[...] = out


def _combine(expert_out, pos, gates, x, mods, final_norm, tok, layer):
    t, d = x.shape
    tm = _pick(tok.ctx_len, (256, 128))
    cond = lambda i: tok.cond_of_tile(i, tm)
    n_ctx_tiles = tok.n_ctx // tm
    grid_spec = pltpu.PrefetchScalarGridSpec(
        num_scalar_prefetch=1,
        grid=(t // tm,),
        in_specs=[
            pl.BlockSpec(memory_space=pl.ANY),
            pl.BlockSpec((tm, LANES), lambda i, pos: (i, 0)),
            pl.BlockSpec((tm, d), lambda i, pos: (i, 0)),
            pl.BlockSpec((None, None, 1, d), lambda i, pos: (layer, cond(i), 0, 5)),
            pl.BlockSpec((1, d), lambda i, pos: (0, 0)),
        ],
        out_specs=[
            pl.BlockSpec((tm, d), lambda i, pos: (jnp.minimum(i, n_ctx_tiles - 1), 0)),
            pl.BlockSpec((tm, d), lambda i, pos: (jnp.maximum(i - n_ctx_tiles, 0), 0)),
        ],
        scratch_shapes=[pltpu.VMEM((TOP_K, tm, d), F32), pltpu.SemaphoreType.DMA(())],
    )
    return pl.pallas_call(
        functools.partial(_combine_kernel, n_ctx_tiles=n_ctx_tiles),
        grid_spec=grid_spec,
        out_shape=[jax.ShapeDtypeStruct((tok.n_ctx, d), F32), jax.ShapeDtypeStruct((tok.n_lat, d), F32)],
        compiler_params=_params("arbitrary"),
        name="moe_combine",
    )(pos, expert_out, gates, x, mods, final_norm)


def _dispatch_plan(idx, n_tok, tm):
    flat_e = idx[:, :TOP_K].reshape(-1)
    n_slots = n_tok * TOP_K
    onehot = (flat_e[:, None] == jnp.arange(N_EXPERTS, dtype=jnp.int32)[None, :]).astype(jnp.int32)
    rank = jnp.sum((jnp.cumsum(onehot, axis=0) - 1) * onehot, axis=1)
    counts = jnp.sum(onehot, axis=0)
    padded = (counts + tm - 1) // tm * tm
    pend = jnp.cumsum(padded)
    pos = ((pend - padded)[flat_e] + rank).astype(jnp.int32)
    n_tiles = n_slots // tm + N_EXPERTS
    src_tok = jnp.zeros((n_tiles * tm,), jnp.int32).at[pos].set(jnp.arange(n_slots, dtype=jnp.int32) // TOP_K)
    tile_start = jnp.arange(n_tiles, dtype=jnp.int32) * tm
    tile_expert = jnp.minimum(jnp.sum(tile_start[:, None] >= pend[None, :], axis=1), N_EXPERTS - 1)
    tile_expert = tile_expert.astype(jnp.int32)
    seg_end = pend - padded + counts
    tile_rows = jnp.clip(seg_end[tile_expert] - tile_start, 0, tm).astype(jnp.int32)
    tile_first = jnp.concatenate([jnp.ones((1,), jnp.int32), (tile_expert[1:] != tile_expert[:-1]).astype(jnp.int32)])
    return pos, src_tok, tile_expert, tile_rows, tile_first


def kernel(x_prompt, x_sample, state_ssm, cache_k, cache_v, c, c_ctx, mod_w, mod_b, norm_w, ssm_in_w, ssm_conv_w, ssm_conv_b, ssm_dt_bias, ssm_a_log, ssm_d, ssm_norm, ssm_out_w, attn_wqkv, attn_q_norm, attn_k_norm, attn_wo, ffn_w1, ffn_w3, ffn_w2, moe_router, moe_w1, moe_w3, moe_w2, final_norm):
    n_ctx_seq, ctx_len, d = x_prompt.shape
    n_lat_seq, lat_len, _ = x_sample.shape
    depth = mod_w.shape[0]
    assert depth == 2 and ssm_in_w.shape[0] == 1 and attn_wqkv.shape[0] == 1
    tok = _Tokens(n_ctx_seq, ctx_len, n_lat_seq, lat_len)
    t = tok.total
    d_inner = ssm_out_w.shape[1]
    ssm_heads = d_inner // SSM_HEAD_DIM
    e_heads = ssm_heads // SSM_GROUPS
    conv_dim = ssm_conv_w.shape[2]
    n_heads = d // HEAD_DIM
    n_kv = n_heads // KV_GROUP
    g, n_state = SSM_GROUPS, SSM_STATE

    x = jnp.concatenate([x_prompt.reshape(tok.n_ctx, d), x_sample.reshape(tok.n_lat, d)], axis=0)
    cond = jnp.zeros((COND_ROWS, d), F32).at[:n_lat_seq].set(c).at[n_lat_seq].set(c_ctx)
    mods = _modulation(cond, mod_w, mod_b)
    norm_w4 = norm_w.reshape(depth * 2, 1, d)

    h = _norm_mod(x, norm_w4, mods, tok, 0, 0, BF16)
    zx_cols = d_inner + conv_dim
    tm_big = _pick(lat_len, (1024, 512, 256))
    proj = _matmul_ws(h, [ssm_in_w[0]], tm=tm_big, tn=_pick(zx_cols, (512, 256, 128)), kind="plain",
                      out_dtype=F32, name="in_proj", n_cols=zx_cols)
    dt_w = 2 * ssm_heads
    dt_raw = _matmul_ws(h, [ssm_in_w[0]], tm=tm_big, tn=dt_w, kind="plain", out_dtype=F32, name="in_proj_dt",
                        n_cols=dt_w, col0=zx_cols)
    xbc = _conv_silu(proj, d_inner, ssm_conv_w[0], ssm_conv_b, tok)
    dt, cs = _dt_prep(dt_raw, ssm_dt_bias.reshape(1, dt_w), ssm_a_log.reshape(1, dt_w))
    grouped = lambda a: a.reshape(t, 2, g, e_heads).transpose(1, 2, 0, 3)
    dtg, csg = grouped(dt), grouped(cs)
    cstg = csg.transpose(0, 1, 3, 2)
    h0 = state_ssm[:, 0].reshape(n_lat_seq, 2, g, e_heads, SSM_HEAD_DIM, n_state)
    h0 = h0.transpose(0, 1, 2, 5, 3, 4).reshape(n_lat_seq, 2, g, n_state, e_heads * SSM_HEAD_DIM)
    y_f, y_b, hfin = _ssd(xbc, dtg, csg, cstg, h0, tok, d_inner)
    d_row = jnp.repeat(ssm_d[0], SSM_HEAD_DIM).reshape(1, d_inner)
    y = _gate_norm(y_f, y_b, xbc, proj, d_row, ssm_norm.reshape(1, d_inner))
    tm = _pick(t, (512, 256))
    x = _matmul_resid(y, ssm_out_w[0].astype(BF16), x, mods, tok, 0, 2, tm, _pick(d, (512, 256, 128)), "out_proj")
    new_state = hfin.reshape(n_ctx_seq, 2, g, n_state, e_heads, SSM_HEAD_DIM)
    new_state = new_state.transpose(0, 1, 2, 4, 5, 3).reshape(n_ctx_seq, 1, 2, ssm_heads, SSM_HEAD_DIM, n_state)

    h = _norm_mod(x, norm_w4, mods, tok, 0, 1, BF16)
    d_ff = ffn_w1.shape[2]
    u = _matmul_ws(h, [ffn_w1[0], ffn_w3[0]], tm=tm_big, tn=_pick(d_ff, (256, 128)), kind="swiglu", out_dtype=BF16,
                   name="ffn_up")
    x = _matmul_resid(u, ffn_w2[0].astype(BF16), x, mods, tok, 0, 5, tm, _pick(d, (512, 256, 128)), "ffn_down")

    h = _norm_mod(x, norm_w4, mods, tok, 1, 0, BF16)
    cos, sin = _rope_tables(lat_len)
    qkv_w = attn_wqkv.shape[2]
    qkv = _matmul_ws(h, [attn_wqkv[0]], tm=tm_big, tn=_pick(qkv_w, (512, 256, 128)), kind="plain", out_dtype=F32,
                     name="qkv_proj")
    k_col0 = n_heads * HEAD_DIM
    v_col0 = k_col0 + n_kv * HEAD_DIM
    (q,) = _qk_norm_rope(qkv, 0, n_heads, attn_q_norm, cos, sin, tok, False)
    k, k_normed = _qk_norm_rope(qkv, k_col0, n_kv, attn_k_norm, cos, sin, tok, True)
    past = cache_k.shape[2]
    o_ctx = _attention(q, k, qkv, v_col0, 0, n_ctx_seq, ctx_len, n_kv)
    o_lat = _attention(q, k, qkv, v_col0, tok.n_ctx, n_lat_seq, lat_len, n_kv,
                       cache_k[:, 0].reshape(n_lat_seq, past, n_kv * HEAD_DIM),
                       cache_v[:, 0].reshape(n_lat_seq, past, n_kv * HEAD_DIM))
    o = jnp.concatenate([o_ctx, o_lat], axis=0)
    tn = _pick(d, (512, 256, 128))
    x = _matmul_ws(o, [attn_wo[0]], tm=tm_big, tn=tn, kind="resid", out_dtype=F32, name="attn_out",
                   extra=(x, mods),
                   extra_specs=(pl.BlockSpec((tm_big, tn), lambda j, i: (i, j)),
                                _gate_spec(tok, 1, 2, tm_big, tn, d, True)))
    new_k = k_normed[:tok.n_ctx].reshape(n_ctx_seq, 1, ctx_len, n_kv, HEAD_DIM)
    new_v = qkv[:tok.n_ctx, v_col0:].reshape(n_ctx_seq, 1, ctx_len, n_kv, HEAD_DIM)

    h = _norm_mod(x, norm_w4, mods, tok, 1, 1, F32)
    idx, gates = _router(h, moe_router[0])
    tmg = 512
    pos, src_tok, tile_expert, tile_rows, tile_first = _dispatch_plan(idx, t, tmg)
    xs = _gather_rows(h, src_tok, tmg)
    d_fe = moe_w1.shape[3]
    u = _expert_swiglu(xs, moe_w1[0], moe_w3[0], tile_expert, tile_rows, tile_first,
                       tmg, _pick(d_fe, (512, 256, 128)))
    eo = _expert_down(u, moe_w2[0].astype(BF16), tile_expert, tile_rows, tmg, _pick(d, (256, 128)))
    y_ctx, y_lat = _combine(eo, pos, gates, x, mods, final_norm.reshape(1, d), tok, 1)

    y_prompt = y_ctx.reshape(n_ctx_seq, ctx_len, d)
    y_sample = y_lat.reshape(n_lat_seq, lat_len, d)
    return (y_prompt, y_sample, new_state, new_k, new_v)
```

```python
import collections
import functools

import numpy as np
import jax
import jax.numpy as jnp
from jax import lax
from jax.experimental import pallas as pl
from jax.experimental.pallas import tpu as pltpu

F32 = jnp.float32
BF16 = jnp.bfloat16

EPS = 1e-6
LOG2_E = 1.4426950408889634
GRID_W = 64
ROPE_THETA = 10000.0
HEAD_DIM = 128
KV_GROUP = 4
SSM_HEAD_DIM = 64
SSM_GROUPS = 8
SSM_STATE = 128
SSD_CHUNK = 128
N_EXPERTS = 8
TOP_K = 2
LANES = 128
COND_ROWS = 16
DMA_UNROLL = 8
VMEM_LIMIT_BYTES = 56 * 1024 * 1024


def _params(*sem):
    return pltpu.CompilerParams(dimension_semantics=sem, vmem_limit_bytes=VMEM_LIMIT_BYTES)


def _silu(x):
    return x * jax.nn.sigmoid(x)


def _bdot(a, b):
    return jnp.dot(a.astype(BF16), b.astype(BF16), preferred_element_type=F32)


def _pick(n, prefs):
    for t in prefs:
        if n % t == 0:
            return t
    return n


class _Tokens:
    def __init__(self, n_ctx_seq, ctx_len, n_lat_seq, lat_len):
        self.n_ctx_seq, self.ctx_len = n_ctx_seq, ctx_len
        self.n_lat_seq, self.lat_len = n_lat_seq, lat_len
        self.n_ctx = n_ctx_seq * ctx_len
        self.n_lat = n_lat_seq * lat_len
        self.total = self.n_ctx + self.n_lat
        assert self.n_ctx % lat_len == 0 and lat_len % ctx_len == 0
        assert ctx_len & (ctx_len - 1) == 0 and lat_len & (lat_len - 1) == 0

    def cond_of_tile(self, i, tm):
        assert self.ctx_len % tm == 0 or tm % self.ctx_len == 0
        assert self.lat_len % tm == 0 and self.n_ctx % tm == 0
        row = i * tm
        return jnp.where(row < self.n_ctx, self.n_lat_seq, (row - self.n_ctx) // self.lat_len)


def _mod_kernel(c_ref, w_ref, b_ref, o_ref):
    o_ref[...] = _bdot(_silu(c_ref[...]), w_ref[...]) + b_ref[...]


def _modulation(cond, mod_w, mod_b):
    depth, d, n = mod_w.shape
    tn = _pick(n, (512, 256, 128))
    out = pl.pallas_call(
        _mod_kernel,
        grid=(depth, n // tn),
        in_specs=[
            pl.BlockSpec((COND_ROWS, d), lambda l, j: (0, 0)),
            pl.BlockSpec((None, d, tn), lambda l, j: (l, 0, j)),
            pl.BlockSpec((None, 1, tn), lambda l, j: (l, 0, j)),
        ],
        out_specs=pl.BlockSpec((None, COND_ROWS, tn), lambda l, j: (l, 0, j)),
        out_shape=jax.ShapeDtypeStruct((depth, COND_ROWS, n), F32),
        compiler_params=_params("arbitrary", "arbitrary"),
        name="modulation",
    )(cond, mod_w, mod_b.reshape(depth, 1, n))
    return out.reshape(depth, COND_ROWS, 1, n)


def _norm_mod_kernel(x_ref, g_ref, sh_ref, sc_ref, o_ref):
    x = x_ref[...]
    y = x * lax.rsqrt(jnp.mean(x * x, axis=-1, keepdims=True) + EPS) * g_ref[...]
    o_ref[...] = (y * (1.0 + sc_ref[...]) + sh_ref[...]).astype(o_ref.dtype)


def _norm_mod(x, norm_w4, mods, tok, layer, which, out_dtype):
    t, d = x.shape
    tm = _pick(tok.ctx_len, (256, 128))
    cond = lambda i: tok.cond_of_tile(i, tm)
    return pl.pallas_call(
        _norm_mod_kernel,
        grid=(t // tm,),
        in_specs=[
            pl.BlockSpec((tm, d), lambda i: (i, 0)),
            pl.BlockSpec((None, 1, d), lambda i: (2 * layer + which, 0, 0)),
            pl.BlockSpec((None, None, 1, d), lambda i: (layer, cond(i), 0, 3 * which)),
            pl.BlockSpec((None, None, 1, d), lambda i: (layer, cond(i), 0, 3 * which + 1)),
        ],
        out_specs=pl.BlockSpec((tm, d), lambda i: (i, 0)),
        out_shape=jax.ShapeDtypeStruct((t, d), out_dtype),
        compiler_params=_params("arbitrary"),
        name="norm_mod",
    )(x, norm_w4, mods, mods)


def _head_norm_rope(x, gain, cos, sin, rope):
    y = x * lax.rsqrt(jnp.mean(x * x, axis=-1, keepdims=True) + EPS) * gain
    lane = lax.broadcasted_iota(jnp.int32, y.shape, 1)
    low = (lane & (HEAD_DIM // 2 - 1)) < HEAD_DIM // 4
    partner = jnp.where(low, pltpu.roll(y, HEAD_DIM - HEAD_DIM // 4, 1), pltpu.roll(y, HEAD_DIM // 4, 1))
    return y, jnp.where(rope, y * cos + partner * sin, y)


def _epilogue(kind, acc, extra_refs, o_ref):
    if kind == "swiglu":
        out = _silu(acc[0]) * acc[1]
    elif kind == "resid":
        r_ref, g_ref = extra_refs
        out = r_ref[...] + g_ref[...] * acc[0]
    else:
        out = acc[0]
    o_ref[...] = out.astype(o_ref.dtype)


def _cast_weights(first, w_refs, wb_refs):
    @pl.when(first)
    def _():
        for w_ref, wb_ref in zip(w_refs, wb_refs):
            wb_ref[...] = w_ref[...].astype(BF16)


def _ws_body(first, kind, x_ref, w_refs, extra_refs, o_ref, wb_refs):
    _cast_weights(first, w_refs, wb_refs)
    x = x_ref[...]
    _epilogue(kind, [jnp.dot(x, wb[...], preferred_element_type=F32) for wb in wb_refs], extra_refs, o_ref)


ROW_SPLITS = 4


def _expert_rows_matmul(kind, n_rows, x_ref, w_refs, o_ref):
    tm = x_ref.shape[0]
    q = tm // ROW_SPLITS
    for b in range(1, ROW_SPLITS + 1):
        m = b * q

        @pl.when((n_rows > m - q) & (n_rows <= m))
        def _(m=m):
            x = x_ref[0:m, :]
            acc = [jnp.dot(x, w[...], preferred_element_type=F32) for w in w_refs]
            _epilogue(kind, acc, (), o_ref.at[0:m, :])
            if m < tm:
                o_ref[m:, :] = jnp.zeros((tm - m, o_ref.shape[1]), o_ref.dtype)

    @pl.when(n_rows == 0)
    def _():
        o_ref[...] = jnp.zeros_like(o_ref)


_N_EXTRA = {"plain": 0, "swiglu": 0, "resid": 2}


def _ws_kernel(*refs, n_w, kind):
    n_extra = _N_EXTRA[kind]
    x_ref, w_refs = refs[0], refs[1:1 + n_w]
    extra_refs = refs[1 + n_w:1 + n_w + n_extra]
    o_ref = refs[1 + n_w + n_extra]
    _ws_body(pl.program_id(1) == 0, kind, x_ref, w_refs, extra_refs, o_ref, refs[2 + n_w + n_extra:])


def _matmul_ws(x, ws, *, tm, tn, kind, out_dtype, name, n_cols=None, col0=0, extra=(), extra_specs=()):
    t, k = x.shape
    n_cols = ws[0].shape[1] if n_cols is None else n_cols
    assert t % tm == 0 and n_cols % tn == 0 and col0 % tn == 0 and len(extra) == _N_EXTRA[kind]
    cb = col0 // tn
    in_specs = [pl.BlockSpec((tm, k), lambda j, i: (i, 0))]
    in_specs += [pl.BlockSpec((k, tn), lambda j, i: (0, cb + j))] * len(ws)
    in_specs += list(extra_specs)
    args = [x, *ws, *extra]
    return pl.pallas_call(
        functools.partial(_ws_kernel, n_w=len(ws), kind=kind),
        grid=(n_cols // tn, t // tm),
        in_specs=in_specs,
        out_specs=pl.BlockSpec((tm, tn), lambda j, i: (i, j)),
        out_shape=jax.ShapeDtypeStruct((t, n_cols), out_dtype),
        scratch_shapes=[pltpu.VMEM((k, tn), BF16)] * len(ws),
        compiler_params=_params("arbitrary", "arbitrary"),
        name=name,
    )(*args)


def _gate_spec(tok, layer, chunk, tm, tn, n, weights_outer):
    nb = n // tn
    if weights_outer:
        return pl.BlockSpec((None, None, 1, tn), lambda j, i: (layer, tok.cond_of_tile(i, tm), 0, chunk * nb + j))
    return pl.BlockSpec((None, None, 1, tn), lambda i, j: (layer, tok.cond_of_tile(i, tm), 0, chunk * nb + j))


def _mm_resid_kernel(x_ref, w_ref, r_ref, g_ref, o_ref):
    _epilogue("resid", [_bdot(x_ref[...], w_ref[...])], (r_ref, g_ref), o_ref)


def _matmul_resid(x, w, res, mods, tok, layer, chunk, tm, tn, name):
    t, k = x.shape
    n = w.shape[1]
    assert t % tm == 0 and n % tn == 0
    return pl.pallas_call(
        _mm_resid_kernel,
        grid=(t // tm, n // tn),
        in_specs=[
            pl.BlockSpec((tm, k), lambda i, j: (i, 0)),
            pl.BlockSpec((k, tn), lambda i, j: (0, j)),
            pl.BlockSpec((tm, tn), lambda i, j: (i, j)),
            _gate_spec(tok, layer, chunk, tm, tn, n, False),
        ],
        out_specs=pl.BlockSpec((tm, tn), lambda i, j: (i, j)),
        out_shape=jax.ShapeDtypeStruct((t, n), F32),
        compiler_params=_params("arbitrary", "arbitrary"),
        name=name,
    )(x, w, res, mods)


def _conv_kernel(x_ref, w_ref, b_ref, o_ref, *, tok):
    i = pl.program_id(0)
    x = x_ref[...]
    rows = x.shape[0]
    seglen = jnp.where(i * rows < tok.n_ctx, tok.ctx_len, tok.lat_len)
    t = lax.broadcasted_iota(jnp.int32, x.shape, 0) & (seglen - 1)
    prev = jnp.where(t == 0, 0.0, pltpu.roll(x, 1, 0))
    nxt = jnp.where(t == seglen - 1, 0.0, pltpu.roll(x, rows - 1, 0))
    w = w_ref[...]
    y = prev * w[0:1] + x * w[1:2] + nxt * w[2:3] + b_ref[...]
    o_ref[...] = _silu(y)


def _conv_silu(proj, col0, conv_w, conv_b, tok):
    t = proj.shape[0]
    n = conv_w.shape[1]
    rows = tok.lat_len
    tc = _pick(n, (512, 256, 128))
    assert col0 % tc == 0
    cb = col0 // tc
    return pl.pallas_call(
        functools.partial(_conv_kernel, tok=tok),
        grid=(t // rows, n // tc),
        in_specs=[
            pl.BlockSpec((rows, tc), lambda i, j: (i, cb + j)),
            pl.BlockSpec((3, tc), lambda i, j: (0, j)),
            pl.BlockSpec((1, tc), lambda i, j: (0, j)),
        ],
        out_specs=pl.BlockSpec((rows, tc), lambda i, j: (i, j)),
        out_shape=jax.ShapeDtypeStruct((t, n), F32),
        compiler_params=_params("arbitrary", "arbitrary"),
        name="conv_silu",
    )(proj, conv_w, conv_b)


def _split3_dot(mat, v):
    hi = v.astype(BF16)
    r1 = v - hi.astype(F32)
    mid = r1.astype(BF16)
    lo = (r1 - mid.astype(F32)).astype(BF16)
    dot = lambda b: jnp.dot(mat, b, preferred_element_type=F32)
    return dot(hi) + dot(mid) + dot(lo)


def _dt_kernel(raw_ref, bias_ref, alog_ref, dt_ref, cs_ref, *, heads):
    x = raw_ref[...] + bias_ref[...]
    dt = jnp.maximum(x, 0.0) + jnp.log1p(jnp.exp(-jnp.abs(x)))
    la = dt * (-jnp.exp(alog_ref[...])) * LOG2_E
    r = lax.broadcasted_iota(jnp.int32, (SSD_CHUNK, SSD_CHUNK), 0)
    c = lax.broadcasted_iota(jnp.int32, (SSD_CHUNK, SSD_CHUNK), 1)
    lower = jnp.where(r >= c, 1.0, 0.0).astype(BF16)
    upper = jnp.where(r <= c, 1.0, 0.0).astype(BF16)
    dt_ref[...] = dt
    cs_ref[:, :heads] = _split3_dot(lower, la[:, :heads])
    cs_ref[:, heads:] = _split3_dot(upper, la[:, heads:])


def _dt_prep(dt_raw, dt_bias, a_log):
    t, w = dt_raw.shape
    spec = pl.BlockSpec((SSD_CHUNK, w), lambda i: (i, 0))
    par = pl.BlockSpec((1, w), lambda i: (0, 0))
    return pl.pallas_call(
        functools.partial(_dt_kernel, heads=w // 2),
        grid=(t // SSD_CHUNK,),
        in_specs=[spec, par, par],
        out_specs=[spec, spec],
        out_shape=[jax.ShapeDtypeStruct((t, w), F32)] * 2,
        compiler_params=_params("arbitrary"),
        name="dt_prep",
    )(dt_raw, dt_bias, a_log)


def _expand_matrix(heads, width):
    r = jnp.arange(3 * heads, dtype=jnp.int32)[:, None] % heads
    j = jnp.arange(heads * width, dtype=jnp.int32)[None, :] // width
    return (r == j).astype(BF16)


def _expand_lanes(v, sel_ref):
    hi = v.astype(BF16)
    r1 = v - hi.astype(F32)
    mid = r1.astype(BF16)
    lo = (r1 - mid.astype(F32)).astype(BF16)
    return jnp.dot(jnp.concatenate([hi, mid, lo], axis=1), sel_ref[...], preferred_element_type=F32)


def _ssd_direction(d, x_ref, b_ref, c_ref, dt_ref, cs_ref, cst_ref, selp_ref, selc_ref, y_ref, h_scr, s_scr, heads):
    p = SSM_HEAD_DIM
    bmat = b_ref[...]
    cmat = c_ref[...].astype(BF16)
    cb = lax.dot_general(cmat, bmat.astype(BF16), (((1,), (1,)), ((), ())), preferred_element_type=F32)
    r = lax.broadcasted_iota(jnp.int32, (SSD_CHUNK, SSD_CHUNK), 0)
    c = lax.broadcasted_iota(jnp.int32, (SSD_CHUNK, SSD_CHUNK), 1)
    mask = (r >= c) if d == 0 else (r <= c)
    end = SSD_CHUNK - 1 if d == 0 else 0
    assert heads % 2 == 0 and 2 * p == SSD_CHUNK
    cs = cs_ref[...]
    cst = cst_ref[...]
    xs = x_ref[...] * _expand_lanes(dt_ref[...], selp_ref)
    col = _expand_lanes(cs, selc_ref)
    y_off = jnp.dot(cmat, h_scr[d].astype(BF16), preferred_element_type=F32)
    low = c < p
    for e in range(0, heads, 2):
        lanes = slice(e * p, (e + 2) * p)
        xs_pair = xs[:, lanes]
        col_a = col[:, e * SSD_CHUNK:(e + 1) * SSD_CHUNK]
        col_b = col[:, (e + 1) * SSD_CHUNK:(e + 2) * SSD_CHUNK]
        y_diag = None
        for col_e, row_e, keep in ((col_a, cst[e:e + 1, :], low), (col_b, cst[e + 1:e + 2, :], ~low)):
            scores = cb * jnp.exp2(jnp.where(mask, col_e - row_e, -jnp.inf))
            part = jnp.dot(scores.astype(BF16), jnp.where(keep, xs_pair, 0.0).astype(BF16),
                           preferred_element_type=F32)
            y_diag = part if y_diag is None else y_diag + part
        cs_pair = jnp.where(low, col_a, col_b)
        y_ref[:, lanes] = y_diag + y_off[:, lanes] * jnp.exp2(cs_pair)
        s_scr[:, lanes] = (xs_pair * jnp.exp2(cs_pair[end:end + 1, :] - cs_pair)).astype(BF16)
    new = jnp.dot(bmat.T.astype(BF16), s_scr[...], preferred_element_type=F32)
    h_scr[d] = h_scr[d] * _expand_lanes(jnp.exp2(cs[end:end + 1, :]), selp_ref) + new


def _ssd_kernel(fr_ref, br_ref, first_ref, last_ref, seq_ref, lat_ref,
                xf_ref, bf_ref, cf_ref, dtf_ref, csf_ref, cstf_ref,
                xb_ref, bb_ref, cb_ref, dtb_ref, csb_ref, cstb_ref,
                h0_ref, selp_ref, selc_ref, yf_ref, yb_ref, hfin_ref, h_scr, s_scr, *, heads):
    s = pl.program_id(1)

    @pl.when(first_ref[s] == 1)
    def _():
        h_scr[...] = jnp.where(lat_ref[s] == 1, h0_ref[...], 0.0)

    _ssd_direction(0, xf_ref, bf_ref, cf_ref, dtf_ref, csf_ref, cstf_ref, selp_ref, selc_ref, yf_ref, h_scr, s_scr, heads)
    _ssd_direction(1, xb_ref, bb_ref, cb_ref, dtb_ref, csb_ref, cstb_ref, selp_ref, selc_ref, yb_ref, h_scr, s_scr, heads)

    @pl.when((last_ref[s] == 1) & (lat_ref[s] == 0))
    def _():
        hfin_ref[...] = h_scr[...]


def _ssd_steps(tok):
    rows = []
    chunk = 0
    for n_seq, length, lat in ((tok.n_ctx_seq, tok.ctx_len, 0), (tok.n_lat_seq, tok.lat_len, 1)):
        nc = length // SSD_CHUNK
        for b in range(n_seq):
            for c in range(nc):
                slot = b if lat else b
                rows.append((chunk + c, chunk + nc - 1 - c, int(c == 0), int(c == nc - 1), slot, lat))
            chunk += nc
    return [jnp.asarray(np.array(col, np.int32)) for col in zip(*rows)]


def _ssd(xbc, dtg, csg, cstg, h0, tok, d_inner):
    t = xbc.shape[0]
    g, n = SSM_GROUPS, SSM_STATE
    gw = d_inner // g
    heads = gw // SSM_HEAD_DIM
    assert n == SSD_CHUNK and gw % n == 0
    b0 = d_inner // n
    c0 = b0 + g
    steps = _ssd_steps(tok)
    n_steps = steps[0].shape[0]
    n_fin = tok.n_ctx_seq

    def spec(shape, fn):
        return pl.BlockSpec(shape, fn)

    def dir_specs(d):
        row = (lambda s, fr, br: fr[s]) if d == 0 else (lambda s, fr, br: br[s])
        return [
            spec((SSD_CHUNK, gw), lambda gi, s, fr, br, *_: (row(s, fr, br), gi)),
            spec((SSD_CHUNK, n), lambda gi, s, fr, br, *_: (row(s, fr, br), b0 + gi)),
            spec((SSD_CHUNK, n), lambda gi, s, fr, br, *_: (row(s, fr, br), c0 + gi)),
            spec((None, None, SSD_CHUNK, heads), lambda gi, s, fr, br, *_: (d, gi, row(s, fr, br), 0)),
            spec((None, None, SSD_CHUNK, heads), lambda gi, s, fr, br, *_: (d, gi, row(s, fr, br), 0)),
            spec((None, None, heads, SSD_CHUNK), lambda gi, s, fr, br, *_: (d, gi, 0, row(s, fr, br))),
        ]

    def h0_map(gi, s, fr, br, first, last, seq, lat):
        return (jnp.where(lat[s] == 1, seq[s], 0), 0, gi, 0, 0)

    def hfin_map(gi, s, fr, br, first, last, seq, lat):
        return (jnp.where(lat[s] == 1, n_fin - 1, seq[s]), 0, gi, 0, 0)

    state_block = (None, 2, None, n, gw)
    sel_p = _expand_matrix(heads, SSM_HEAD_DIM)
    sel_c = _expand_matrix(heads, SSD_CHUNK)
    const = lambda a: spec(a.shape, lambda gi, s, *_: (0, 0))
    grid_spec = pltpu.PrefetchScalarGridSpec(
        num_scalar_prefetch=6,
        grid=(g, n_steps),
        in_specs=dir_specs(0) + dir_specs(1) + [spec(state_block, h0_map), const(sel_p), const(sel_c)],
        out_specs=[
            spec((SSD_CHUNK, gw), lambda gi, s, fr, br, *_: (fr[s], gi)),
            spec((SSD_CHUNK, gw), lambda gi, s, fr, br, *_: (br[s], gi)),
            spec(state_block, hfin_map),
        ],
        scratch_shapes=[pltpu.VMEM((2, n, gw), F32), pltpu.VMEM((SSD_CHUNK, gw), BF16)],
    )
    return pl.pallas_call(
        functools.partial(_ssd_kernel, heads=heads),
        grid_spec=grid_spec,
        out_shape=[
            jax.ShapeDtypeStruct((t, d_inner), F32),
            jax.ShapeDtypeStruct((t, d_inner), F32),
            jax.ShapeDtypeStruct((n_fin, 2, g, n, gw), F32),
        ],
        compiler_params=_params("arbitrary", "arbitrary"),
        name="ssd_scan",
    )(*steps, xbc, xbc, xbc, dtg, csg, cstg, xbc, xbc, xbc, dtg, csg, cstg, h0, sel_p, sel_c)


def _gate_norm_kernel(yf_ref, yb_ref, xv_ref, z_ref, d_ref, g_ref, o_ref):
    y = yf_ref[...] + yb_ref[...] + d_ref[...] * xv_ref[...]
    y = y * _silu(z_ref[...])
    y = y * lax.rsqrt(jnp.mean(y * y, axis=-1, keepdims=True) + EPS) * g_ref[...]
    o_ref[...] = y.astype(o_ref.dtype)


def _gate_norm(y_f, y_b, xbc, proj, d_row, norm_row):
    t, d_inner = y_f.shape
    gw = d_inner // SSM_GROUPS
    tm = 256
    blk = pl.BlockSpec((tm, gw), lambda i, j: (i, j))
    par = pl.BlockSpec((1, gw), lambda i, j: (0, j))
    return pl.pallas_call(
        _gate_norm_kernel,
        grid=(t // tm, SSM_GROUPS),
        in_specs=[blk, blk, blk, blk, par, par],
        out_specs=blk,
        out_shape=jax.ShapeDtypeStruct((t, d_inner), BF16),
        compiler_params=_params("arbitrary", "arbitrary"),
        name="gate_norm",
    )(y_f, y_b, xbc, proj, d_row, norm_row)


def _qk_norm_rope_kernel(x_ref, g_ref, cos_ref, sin_ref, o_ref, *rest, n_ctx_tiles):
    rope = pl.program_id(0) >= n_ctx_tiles
    y, out = _head_norm_rope(x_ref[...], g_ref[...], cos_ref[...], sin_ref[...], rope)
    if rest:
        rest[0][...] = y
    o_ref[...] = out.astype(o_ref.dtype)


def _qk_norm_rope(qkv, col0, n_heads, gain, cos, sin, tok, emit_f32):
    t = qkv.shape[0]
    rows = tok.lat_len
    hb = col0 // HEAD_DIM
    blk_in = pl.BlockSpec((rows, HEAD_DIM), lambda i, j: (i, hb + j))
    blk = pl.BlockSpec((rows, HEAD_DIM), lambda i, j: (i, j))
    par = pl.BlockSpec((1, HEAD_DIM), lambda i, j: (0, 0))
    tab = pl.BlockSpec((rows, HEAD_DIM), lambda i, j: (0, 0))
    shapes = [jax.ShapeDtypeStruct((t, n_heads * HEAD_DIM), BF16)]
    if emit_f32:
        shapes.append(jax.ShapeDtypeStruct((t, n_heads * HEAD_DIM), F32))
    return pl.pallas_call(
        functools.partial(_qk_norm_rope_kernel, n_ctx_tiles=tok.n_ctx // rows),
        grid=(t // rows, n_heads),
        in_specs=[blk_in, par, tab, tab],
        out_specs=[blk] * len(shapes),
        out_shape=shapes,
        compiler_params=_params("arbitrary", "arbitrary"),
        name="qk_norm_rope",
    )(qkv, gain, cos, sin)


def _attn_kernel(q_ref, k_ref, v_ref, *rest, has_cache):
    if has_cache:
        kc_ref, vc_ref, o_ref = rest
    else:
        (o_ref,) = rest
    c = HEAD_DIM ** -0.5 * LOG2_E
    nt = (((1,), (1,)), ((), ()))
    k = k_ref[...]
    v = v_ref[...].astype(BF16)
    if has_cache:
        kc = kc_ref[...].astype(BF16)
        vc = vc_ref[...].astype(BF16)
    for h in range(KV_GROUP):
        lanes = slice(h * HEAD_DIM, (h + 1) * HEAD_DIM)
        q = q_ref[:, lanes]
        s = lax.dot_general(q, k, nt, preferred_element_type=F32)
        m = jnp.max(s, axis=-1, keepdims=True)
        if has_cache:
            sc = lax.dot_general(q, kc, nt, preferred_element_type=F32)
            m = jnp.maximum(m, jnp.max(sc, axis=-1, keepdims=True))
            ec = jnp.exp2((sc - m) * c)
        e = jnp.exp2((s - m) * c)
        den = jnp.sum(e, axis=-1, keepdims=True)
        if has_cache:
            den = den + jnp.sum(ec, axis=-1, keepdims=True)
        inv = 1.0 / den
        acc = jnp.dot((e * inv).astype(BF16), v, preferred_element_type=F32)
        if has_cache:
            acc = acc + jnp.dot((ec * inv).astype(BF16), vc, preferred_element_type=F32)
        o_ref[:, lanes] = acc.astype(o_ref.dtype)


def _attention(q, k, qkv, v_col0, row0, n_seq, seq_len, n_kv, cache_k=None, cache_v=None):
    tq = _pick(seq_len, (256, 128))
    nq = seq_len // tq
    sb = row0 // seq_len
    vb = v_col0 // HEAD_DIM
    qw = KV_GROUP * HEAD_DIM
    in_specs = [
        pl.BlockSpec((tq, qw), lambda b, h, i: ((sb + b) * nq + i, h)),
        pl.BlockSpec((seq_len, HEAD_DIM), lambda b, h, i: (sb + b, h)),
        pl.BlockSpec((seq_len, HEAD_DIM), lambda b, h, i: (sb + b, vb + h)),
    ]
    args = [q, k, qkv]
    if cache_k is not None:
        past = cache_k.shape[1]
        cspec = pl.BlockSpec((None, past, HEAD_DIM), lambda b, h, i: (b, 0, h))
        in_specs += [cspec, cspec]
        args += [cache_k, cache_v]
    return pl.pallas_call(
        functools.partial(_attn_kernel, has_cache=cache_k is not None),
        grid=(n_seq, n_kv, nq),
        in_specs=in_specs,
        out_specs=pl.BlockSpec((tq, qw), lambda b, h, i: (b * nq + i, h)),
        out_shape=jax.ShapeDtypeStruct((n_seq * seq_len, n_kv * qw), BF16),
        compiler_params=_params("arbitrary", "arbitrary", "arbitrary"),
        name="attention",
    )(*args)


def _rope_tables(n_tokens):
    t = jnp.arange(n_tokens)
    row = (t // GRID_W).astype(F32)
    col = (t % GRID_W).astype(F32)
    axis_dim = HEAD_DIM // 2
    inv_freq = ROPE_THETA ** (-jnp.arange(0, axis_dim, 2, dtype=F32) / axis_dim)
    ang = jnp.concatenate([row[:, None] * inv_freq] * 2 + [col[:, None] * inv_freq] * 2, axis=-1)
    sign = jnp.tile(jnp.repeat(jnp.array([-1.0, 1.0], F32), HEAD_DIM // 4), 2)
    return jnp.cos(ang), jnp.sin(ang) * sign


def _router_kernel(x_ref, w_ref, idx_ref, gate_ref):
    logits = _bdot(x_ref[...], w_ref[...])
    lane = lax.broadcasted_iota(jnp.int32, logits.shape, 1)
    neg = -jnp.inf
    l1 = jnp.where(lane < N_EXPERTS, logits, neg)
    m1 = jnp.max(l1, axis=-1, keepdims=True)
    i1 = jnp.min(jnp.where(l1 == m1, lane, LANES), axis=-1, keepdims=True)
    l2 = jnp.where(lane == i1, neg, l1)
    m2 = jnp.max(l2, axis=-1, keepdims=True)
    i2 = jnp.min(jnp.where(l2 == m2, lane, LANES), axis=-1, keepdims=True)
    e2 = jnp.exp(m2 - m1)
    den = 1.0 + e2
    idx_ref[...] = jnp.where(lane == 0, i1, jnp.where(lane == 1, i2, 0))
    gate_ref[...] = jnp.where(lane == 0, 1.0 / den, jnp.where(lane == 1, e2 / den, 0.0))


def _router(h, router_w):
    t, d = h.shape
    tm = 256
    w = jnp.zeros((d, LANES), F32).at[:, :N_EXPERTS].set(router_w)
    out = pl.BlockSpec((tm, LANES), lambda i: (i, 0))
    return pl.pallas_call(
        _router_kernel,
        grid=(t // tm,),
        in_specs=[pl.BlockSpec((tm, d), lambda i: (i, 0)), pl.BlockSpec((d, LANES), lambda i: (0, 0))],
        out_specs=[out, out],
        out_shape=[jax.ShapeDtypeStruct((t, LANES), jnp.int32), jax.ShapeDtypeStruct((t, LANES), F32)],
        compiler_params=_params("arbitrary"),
        name="router",
    )(h, w)


def _row_copy(src_hbm, row, dst, slot, sem):
    return pltpu.make_async_copy(src_hbm.at[pl.ds(row, 1), :], dst.at[pl.ds(slot, 1), :], sem)


def _gather_kernel(idx_ref, src_hbm, o_ref, buf, sem):
    rows = buf.shape[0]
    base = pl.program_id(0) * rows

    def start(r, carry):
        _row_copy(src_hbm, idx_ref[base + r], buf, r, sem).start()
        return carry

    def wait(r, carry):
        _row_copy(src_hbm, 0, buf, r, sem).wait()
        return carry

    lax.fori_loop(0, rows, start, 0, unroll=DMA_UNROLL)
    lax.fori_loop(0, rows, wait, 0, unroll=DMA_UNROLL)
    o_ref[...] = buf[...].astype(o_ref.dtype)


def _gather_rows(src, idx, rows):
    n = idx.shape[0]
    d = src.shape[1]
    grid_spec = pltpu.PrefetchScalarGridSpec(
        num_scalar_prefetch=1,
        grid=(n // rows,),
        in_specs=[pl.BlockSpec(memory_space=pl.ANY)],
        out_specs=pl.BlockSpec((rows, d), lambda i, idx: (i, 0)),
        scratch_shapes=[pltpu.VMEM((rows, d), src.dtype), pltpu.SemaphoreType.DMA(())],
    )
    return pl.pallas_call(
        _gather_kernel,
        grid_spec=grid_spec,
        out_shape=jax.ShapeDtypeStruct((n, d), BF16),
        compiler_params=_params("arbitrary"),
        name="moe_gather",
    )(idx, src)


def _expert_swiglu_kernel(te_ref, tr_ref, tf_ref, sid_ref, snext_ref, meta_ref, x_ref, w1_hbm, w3_hbm, o_ref,
                          wf_buf, wb1_ref, wb3_ref, sem):
    j, i = pl.program_id(0), pl.program_id(1)
    tn = wb1_ref.shape[1]
    n_seg, first_expert = meta_ref[0], meta_ref[1]

    def copies(expert, col_tile, slot):
        cols = pl.ds(pl.multiple_of(col_tile * tn, tn), tn)
        return [pltpu.make_async_copy(w_hbm.at[expert, :, cols], wf_buf.at[slot, k], sem.at[slot, k])
                for k, w_hbm in enumerate((w1_hbm, w3_hbm))]

    @pl.when((tf_ref[i] == 1) & (tr_ref[i] > 0))
    def _():
        seg = j * n_seg + sid_ref[i]
        slot = seg % 2

        @pl.when(seg == 0)
        def _():
            for cp in copies(te_ref[i], j, slot):
                cp.start()

        for cp in copies(te_ref[i], j, slot):
            cp.wait()

        @pl.when(snext_ref[i] >= 0)
        def _():
            for cp in copies(snext_ref[i], j, 1 - slot):
                cp.start()

        @pl.when((snext_ref[i] < 0) & (j + 1 < pl.num_programs(0)))
        def _():
            for cp in copies(first_expert, j + 1, 1 - slot):
                cp.start()

        wb1_ref[...] = wf_buf[slot, 0].astype(BF16)
        wb3_ref[...] = wf_buf[slot, 1].astype(BF16)

    _expert_rows_matmul("swiglu", tr_ref[i], x_ref, (wb1_ref, wb3_ref), o_ref)


def _expert_swiglu(x, w1, w3, plan, tm, tn):
    cap, k = x.shape
    f = w1.shape[2]
    grid_spec = pltpu.PrefetchScalarGridSpec(
        num_scalar_prefetch=6,
        grid=(f // tn, cap // tm),
        in_specs=[
            pl.BlockSpec((tm, k), lambda j, i, *_: (i, 0)),
            pl.BlockSpec(memory_space=pl.ANY),
            pl.BlockSpec(memory_space=pl.ANY),
        ],
        out_specs=pl.BlockSpec((tm, tn), lambda j, i, *_: (i, j)),
        scratch_shapes=[
            pltpu.VMEM((2, 2, k, tn), F32),
            pltpu.VMEM((k, tn), BF16),
            pltpu.VMEM((k, tn), BF16),
            pltpu.SemaphoreType.DMA((2, 2)),
        ],
    )
    return pl.pallas_call(
        _expert_swiglu_kernel,
        grid_spec=grid_spec,
        out_shape=jax.ShapeDtypeStruct((cap, f), BF16),
        compiler_params=_params("arbitrary", "arbitrary"),
        name="expert_swiglu",
    )(plan.tile_expert, plan.tile_rows, plan.tile_first, plan.seg_id, plan.seg_next, plan.seg_meta, x, w1, w3)


def _expert_down_kernel(te_ref, tr_ref, x_ref, w_ref, o_ref):
    _expert_rows_matmul("plain", tr_ref[pl.program_id(0)], x_ref, (w_ref,), o_ref)


def _expert_down(x, w2, tile_expert, tile_rows, tm, tn):
    cap, f = x.shape
    d = w2.shape[2]
    grid_spec = pltpu.PrefetchScalarGridSpec(
        num_scalar_prefetch=2,
        grid=(cap // tm, d // tn),
        in_specs=[
            pl.BlockSpec((tm, f), lambda i, j, te, tv: (i, 0)),
            pl.BlockSpec((None, f, tn), lambda i, j, te, tv: (te[i], 0, j)),
        ],
        out_specs=pl.BlockSpec((tm, tn), lambda i, j, te, tv: (i, j)),
    )
    return pl.pallas_call(
        _expert_down_kernel,
        grid_spec=grid_spec,
        out_shape=jax.ShapeDtypeStruct((cap, d), F32),
        compiler_params=_params("arbitrary", "arbitrary"),
        name="expert_down",
    )(tile_expert, tile_rows, x, w2)


def _combine_kernel(pos_ref, eo_hbm, gate_ref, x_ref, g_ref, fn_ref, octx_ref, olat_ref, buf, sem, *, n_ctx_tiles):
    rows = x_ref.shape[0]
    tile = pl.program_id(0)
    base = tile * rows * TOP_K

    def start(r, carry):
        for k in range(TOP_K):
            _row_copy(eo_hbm, pos_ref[base + TOP_K * r + k], buf.at[k], r, sem).start()
        return carry

    def wait(r, carry):
        for k in range(TOP_K):
            _row_copy(eo_hbm, 0, buf.at[k], r, sem).wait()
        return carry

    lax.fori_loop(0, rows, start, 0, unroll=DMA_UNROLL)
    lax.fori_loop(0, rows, wait, 0, unroll=DMA_UNROLL)
    gate = gate_ref[...]
    y = buf[0] * gate[:, 0:1] + buf[1] * gate[:, 1:2]
    x = x_ref[...] + g_ref[...] * y
    out = x * lax.rsqrt(jnp.mean(x * x, axis=-1, keepdims=True) + EPS) * fn_ref[...]

    @pl.when(tile < n_ctx_tiles)
    def _():
        octx_ref[...] = out

    @pl.when(tile >= n_ctx_tiles)
    def _():
        olat_ref[...] = out


def _combine(expert_out, pos, gates, x, mods, final_norm, tok, layer):
    t, d = x.shape
    tm = _pick(tok.ctx_len, (256, 128))
    cond = lambda i: tok.cond_of_tile(i, tm)
    n_ctx_tiles = tok.n_ctx // tm
    grid_spec = pltpu.PrefetchScalarGridSpec(
        num_scalar_prefetch=1,
        grid=(t // tm,),
        in_specs=[
            pl.BlockSpec(memory_space=pl.ANY),
            pl.BlockSpec((tm, LANES), lambda i, pos: (i, 0)),
            pl.BlockSpec((tm, d), lambda i, pos: (i, 0)),
            pl.BlockSpec((None, None, 1, d), lambda i, pos: (layer, cond(i), 0, 5)),
            pl.BlockSpec((1, d), lambda i, pos: (0, 0)),
        ],
        out_specs=[
            pl.BlockSpec((tm, d), lambda i, pos: (jnp.minimum(i, n_ctx_tiles - 1), 0)),
            pl.BlockSpec((tm, d), lambda i, pos: (jnp.maximum(i - n_ctx_tiles, 0), 0)),
        ],
        scratch_shapes=[pltpu.VMEM((TOP_K, tm, d), F32), pltpu.SemaphoreType.DMA(())],
    )
    return pl.pallas_call(
        functools.partial(_combine_kernel, n_ctx_tiles=n_ctx_tiles),
        grid_spec=grid_spec,
        out_shape=[jax.ShapeDtypeStruct((tok.n_ctx, d), F32), jax.ShapeDtypeStruct((tok.n_lat, d), F32)],
        compiler_params=_params("arbitrary"),
        name="moe_combine",
    )(pos, expert_out, gates, x, mods, final_norm)


_Plan = collections.namedtuple(
    "_Plan", "pos src_tok tile_expert tile_rows tile_first seg_id seg_next seg_meta")


def _dispatch_plan(idx, n_tok, tm):
    flat_e = idx[:, :TOP_K].reshape(-1)
    n_slots = n_tok * TOP_K
    onehot = (flat_e[:, None] == jnp.arange(N_EXPERTS, dtype=jnp.int32)[None, :]).astype(jnp.int32)
    rank = jnp.sum((jnp.cumsum(onehot, axis=0) - 1) * onehot, axis=1)
    counts = jnp.sum(onehot, axis=0)
    padded = (counts + tm - 1) // tm * tm
    pend = jnp.cumsum(padded)
    pos = ((pend - padded)[flat_e] + rank).astype(jnp.int32)
    n_tiles = n_slots // tm + N_EXPERTS
    src_tok = jnp.zeros((n_tiles * tm,), jnp.int32).at[pos].set(jnp.arange(n_slots, dtype=jnp.int32) // TOP_K)
    tile_start = jnp.arange(n_tiles, dtype=jnp.int32) * tm
    tile_expert = jnp.minimum(jnp.sum(tile_start[:, None] >= pend[None, :], axis=1), N_EXPERTS - 1)
    tile_expert = tile_expert.astype(jnp.int32)
    seg_end = pend - padded + counts
    tile_rows = jnp.clip(seg_end[tile_expert] - tile_start, 0, tm).astype(jnp.int32)
    tile_first = jnp.concatenate([jnp.ones((1,), jnp.int32), (tile_expert[1:] != tile_expert[:-1]).astype(jnp.int32)])
    present = counts > 0
    seg_of_expert = jnp.cumsum(present.astype(jnp.int32)) - 1
    following, nearest = [], jnp.int32(-1)
    for e in reversed(range(N_EXPERTS)):
        following.append(nearest)
        nearest = jnp.where(present[e], jnp.int32(e), nearest)
    next_of_expert = jnp.stack(following[::-1])
    seg_meta = jnp.stack([jnp.sum(present.astype(jnp.int32)), nearest]).astype(jnp.int32)
    return _Plan(pos, src_tok, tile_expert, tile_rows, tile_first,
                 seg_of_expert[tile_expert].astype(jnp.int32), next_of_expert[tile_expert].astype(jnp.int32), seg_meta)


def kernel(x_prompt, x_sample, state_ssm, cache_k, cache_v, c, c_ctx, mod_w, mod_b, norm_w, ssm_in_w, ssm_conv_w, ssm_conv_b, ssm_dt_bias, ssm_a_log, ssm_d, ssm_norm, ssm_out_w, attn_wqkv, attn_q_norm, attn_k_norm, attn_wo, ffn_w1, ffn_w3, ffn_w2, moe_router, moe_w1, moe_w3, moe_w2, final_norm):
    n_ctx_seq, ctx_len, d = x_prompt.shape
    n_lat_seq, lat_len, _ = x_sample.shape
    depth = mod_w.shape[0]
    assert depth == 2 and ssm_in_w.shape[0] == 1 and attn_wqkv.shape[0] == 1
    tok = _Tokens(n_ctx_seq, ctx_len, n_lat_seq, lat_len)
    t = tok.total
    d_inner = ssm_out_w.shape[1]
    ssm_heads = d_inner // SSM_HEAD_DIM
    e_heads = ssm_heads // SSM_GROUPS
    conv_dim = ssm_conv_w.shape[2]
    n_heads = d // HEAD_DIM
    n_kv = n_heads // KV_GROUP
    g, n_state = SSM_GROUPS, SSM_STATE

    x = jnp.concatenate([x_prompt.reshape(tok.n_ctx, d), x_sample.reshape(tok.n_lat, d)], axis=0)
    cond = jnp.zeros((COND_ROWS, d), F32).at[:n_lat_seq].set(c).at[n_lat_seq].set(c_ctx)
    mods = _modulation(cond, mod_w, mod_b)
    norm_w4 = norm_w.reshape(depth * 2, 1, d)

    h = _norm_mod(x, norm_w4, mods, tok, 0, 0, BF16)
    zx_cols = d_inner + conv_dim
    tm_big = _pick(lat_len, (1024, 512, 256))
    proj = _matmul_ws(h, [ssm_in_w[0]], tm=tm_big, tn=_pick(zx_cols, (512, 256, 128)), kind="plain",
                      out_dtype=F32, name="in_proj", n_cols=zx_cols)
    dt_w = 2 * ssm_heads
    dt_raw = _matmul_ws(h, [ssm_in_w[0]], tm=tm_big, tn=dt_w, kind="plain", out_dtype=F32, name="in_proj_dt",
                        n_cols=dt_w, col0=zx_cols)
    xbc = _conv_silu(proj, d_inner, ssm_conv_w[0], ssm_conv_b, tok)
    dt, cs = _dt_prep(dt_raw, ssm_dt_bias.reshape(1, dt_w), ssm_a_log.reshape(1, dt_w))
    grouped = lambda a: a.reshape(t, 2, g, e_heads).transpose(1, 2, 0, 3)
    dtg, csg = grouped(dt), grouped(cs)
    cstg = csg.transpose(0, 1, 3, 2)
    h0 = state_ssm[:, 0].reshape(n_lat_seq, 2, g, e_heads, SSM_HEAD_DIM, n_state)
    h0 = h0.transpose(0, 1, 2, 5, 3, 4).reshape(n_lat_seq, 2, g, n_state, e_heads * SSM_HEAD_DIM)
    y_f, y_b, hfin = _ssd(xbc, dtg, csg, cstg, h0, tok, d_inner)
    d_row = jnp.repeat(ssm_d[0], SSM_HEAD_DIM).reshape(1, d_inner)
    y = _gate_norm(y_f, y_b, xbc, proj, d_row, ssm_norm.reshape(1, d_inner))
    tm = _pick(t, (512, 256))
    x = _matmul_resid(y, ssm_out_w[0].astype(BF16), x, mods, tok, 0, 2, tm, _pick(d, (512, 256, 128)), "out_proj")
    new_state = hfin.reshape(n_ctx_seq, 2, g, n_state, e_heads, SSM_HEAD_DIM)
    new_state = new_state.transpose(0, 1, 2, 4, 5, 3).reshape(n_ctx_seq, 1, 2, ssm_heads, SSM_HEAD_DIM, n_state)

    h = _norm_mod(x, norm_w4, mods, tok, 0, 1, BF16)
    d_ff = ffn_w1.shape[2]
    u = _matmul_ws(h, [ffn_w1[0], ffn_w3[0]], tm=tm_big, tn=_pick(d_ff, (256, 128)), kind="swiglu", out_dtype=BF16,
                   name="ffn_up")
    x = _matmul_resid(u, ffn_w2[0].astype(BF16), x, mods, tok, 0, 5, tm, _pick(d, (512, 256, 128)), "ffn_down")

    h = _norm_mod(x, norm_w4, mods, tok, 1, 0, BF16)
    cos, sin = _rope_tables(lat_len)
    qkv_w = attn_wqkv.shape[2]
    qkv = _matmul_ws(h, [attn_wqkv[0]], tm=tm_big, tn=_pick(qkv_w, (512, 256, 128)), kind="plain", out_dtype=F32,
                     name="qkv_proj")
    k_col0 = n_heads * HEAD_DIM
    v_col0 = k_col0 + n_kv * HEAD_DIM
    (q,) = _qk_norm_rope(qkv, 0, n_heads, attn_q_norm, cos, sin, tok, False)
    k, k_normed = _qk_norm_rope(qkv, k_col0, n_kv, attn_k_norm, cos, sin, tok, True)
    past = cache_k.shape[2]
    o_ctx = _attention(q, k, qkv, v_col0, 0, n_ctx_seq, ctx_len, n_kv)
    o_lat = _attention(q, k, qkv, v_col0, tok.n_ctx, n_lat_seq, lat_len, n_kv,
                       cache_k[:, 0].reshape(n_lat_seq, past, n_kv * HEAD_DIM),
                       cache_v[:, 0].reshape(n_lat_seq, past, n_kv * HEAD_DIM))
    o = jnp.concatenate([o_ctx, o_lat], axis=0)
    tn = _pick(d, (512, 256, 128))
    x = _matmul_ws(o, [attn_wo[0]], tm=tm_big, tn=tn, kind="resid", out_dtype=F32, name="attn_out",
                   extra=(x, mods),
                   extra_specs=(pl.BlockSpec((tm_big, tn), lambda j, i: (i, j)),
                                _gate_spec(tok, 1, 2, tm_big, tn, d, True)))
    new_k = k_normed[:tok.n_ctx].reshape(n_ctx_seq, 1, ctx_len, n_kv, HEAD_DIM)
    new_v = qkv[:tok.n_ctx, v_col0:].reshape(n_ctx_seq, 1, ctx_len, n_kv, HEAD_DIM)

    h = _norm_mod(x, norm_w4, mods, tok, 1, 1, F32)
    idx, gates = _router(h, moe_router[0])
    tmg = 512
    plan = _dispatch_plan(idx, t, tmg)
    xs = _gather_rows(h, plan.src_tok, tmg)
    d_fe = moe_w1.shape[3]
    u = _expert_swiglu(xs, moe_w1[0], moe_w3[0], plan, tmg, _pick(d_fe, (512, 256, 128)))
    eo = _expert_down(u, moe_w2[0].astype(BF16), plan.tile_expert, plan.tile_rows, tmg, _pick(d, (256, 128)))
    y_ctx, y_lat = _combine(eo, plan.pos, gates, x, mods, final_norm.reshape(1, d), tok, 1)

    y_prompt = y_ctx.reshape(n_ctx_seq, ctx_len, d)
    y_sample = y_lat.reshape(n_lat_seq, lat_len, d)
    return (y_prompt, y_sample, new_state, new_k, new_v)
```
